```python
import math
import jax
import jax.numpy as jnp
from jax import lax
import numpy as np

D_MODEL = 1024
BATCH = 16
SEQ = 2048
DEPTH = 2

CHUNK = 64
MEM_LEN = 256
EPS = 1e-6
CONV_WIDTH = 4

LRU_WIDTH = 1024
LRU_HEADS = 8
LRU_HEAD_DIM = LRU_WIDTH // LRU_HEADS
LRU_C = 8.0

SGU_WIDTH = 1024
SGU_GROUPS = 4
SGU_GROUP_DIM = SGU_WIDTH // SGU_GROUPS
SGU_BLOCK = 128

SSM_WIDTH = 1024
SSM_HEAD_DIM = 64
SSM_HEADS = SSM_WIDTH // SSM_HEAD_DIM
SSM_GROUPS = 4
SSM_HEADS_PER_GROUP = SSM_HEADS // SSM_GROUPS
SSM_STATE = 128
SSM_CHUNK = CHUNK
SSM_CONV_DIM = SSM_WIDTH + 2 * SSM_GROUPS * SSM_STATE

N_BRANCHES = 3

OFF_YA = LRU_WIDTH
OFF_UV = 2 * LRU_WIDTH
OFF_Z = OFF_UV + 2 * SGU_WIDTH
OFF_XBC = OFF_Z + SSM_WIDTH
OFF_DT = OFF_XBC + SSM_CONV_DIM
OFF_GATE = OFF_DT + SSM_HEADS
IN_DIM = OFF_GATE + N_BRANCHES * D_MODEL

XA_HEADS = 4
XA_HEAD_DIM = D_MODEL // XA_HEADS

N_EXPERT_GROUPS = 4
EXPERTS_PER_GROUP = 8
N_EXPERTS = N_EXPERT_GROUPS * EXPERTS_PER_GROUP
TOP_K_IN_GROUP = 2
D_EXPERT = 512
MOE_BLOCK = 256

kernel_name = 'hybrid_rglru_gmlp_ssd_hmoe_encoder'


def rmsnorm(x, g):
    xf = x.astype(jnp.float32)
    y = xf * lax.rsqrt(jnp.mean(xf * xf, axis=-1, keepdims=True) + EPS)
    return (y * g.astype(jnp.float32)).astype(x.dtype)


def layernorm(x, g):
    xf = x.astype(jnp.float32)
    xc = xf - jnp.mean(xf, axis=-1, keepdims=True)
    y = xc * lax.rsqrt(jnp.mean(xc * xc, axis=-1, keepdims=True) + EPS)
    return (y * g.astype(jnp.float32)).astype(x.dtype)


def causal_depthwise_conv(x, w, b):
    k = w.shape[0]
    y = lax.conv_general_dilated(
        x, w[:, None, :].astype(x.dtype), window_strides=(1,),
        padding=[(k - 1, 0)], dimension_numbers=('NWC', 'WIO', 'NWC'),
        feature_group_count=x.shape[-1])
    return y + b.astype(x.dtype)


def rglru_branch(xa, ya, conv_w, conv_b, gx_w, gx_b, ga_w, ga_b, lam):
    bsz, s, _ = xa.shape
    xc = causal_depthwise_conv(xa, conv_w, conv_b)
    xh = xc.reshape(bsz, s, LRU_HEADS, LRU_HEAD_DIM)
    gate_x = jax.nn.sigmoid(jnp.einsum('bshi,hij->bshj', xh, gx_w) + gx_b).reshape(bsz, s, LRU_WIDTH)
    gate_a = jax.nn.sigmoid(jnp.einsum('bshi,hij->bshj', xh, ga_w) + ga_b).reshape(bsz, s, LRU_WIDTH)
    log_a = (-LRU_C * gate_a.astype(jnp.float32)) * jax.nn.softplus(-lam.astype(jnp.float32))
    a = jnp.exp(log_a)
    mult = jnp.sqrt(jnp.maximum(1.0 - jnp.exp(2.0 * log_a), 0.0))
    b = xc.astype(jnp.float32) * gate_x.astype(jnp.float32) * mult

    def combine(left, right):
        return left[0] * right[0], right[0] * left[1] + right[1]

    _, h = lax.associative_scan(combine, (a, b), axis=1)
    return h.astype(xa.dtype) * jax.nn.gelu(ya)


def sgu_branch(uv, norm_g, w_s, b_s):
    z = jax.nn.gelu(uv)
    u, v = jnp.split(z, 2, axis=-1)
    v = layernorm(v, norm_g)
    bsz, s, _ = v.shape
    nb = s // SGU_BLOCK
    vb = v.reshape(bsz, nb, SGU_BLOCK, SGU_GROUPS, SGU_GROUP_DIM)
    chunk_id = jnp.arange(SGU_BLOCK) // CHUNK
    mask = chunk_id[None, :] <= chunk_id[:, None]
    w = jnp.where(mask[None], w_s, 0.0).astype(v.dtype)
    mixed = jnp.einsum('gts,bnsgd->bntgd', w, vb) + b_s.T.astype(v.dtype)[None, None, :, :, None]
    return u * mixed.reshape(bsz, s, SGU_WIDTH)


def ssd_branch(z, xbc, dt, conv_w, conv_b, dt_bias, a_log, d_skip, norm_g):
    bsz, s, _ = z.shape
    g_, r_, p_, n_, l_ = SSM_GROUPS, SSM_HEADS_PER_GROUP, SSM_HEAD_DIM, SSM_STATE, SSM_CHUNK
    nc = s // l_
    xbc = jax.nn.silu(causal_depthwise_conv(xbc, conv_w, conv_b))
    xs, bm, cm = jnp.split(xbc, [SSM_WIDTH, SSM_WIDTH + g_ * n_], axis=-1)
    x = xs.reshape(bsz, nc, l_, g_, r_, p_).astype(jnp.float32)
    bm = bm.reshape(bsz, nc, l_, g_, n_).astype(jnp.float32)
    cm = cm.reshape(bsz, nc, l_, g_, n_).astype(jnp.float32)
    dt = jax.nn.softplus(dt.astype(jnp.float32) + dt_bias.astype(jnp.float32)).reshape(bsz, nc, l_, g_, r_)
    a = -jnp.exp(a_log.astype(jnp.float32)).reshape(g_, r_)
    cs = jnp.cumsum(dt * a, axis=2)
    seg = cs[:, :, :, None] - cs[:, :, None, :]
    causal = (jnp.arange(l_)[:, None] >= jnp.arange(l_)[None, :])[None, None, :, :, None, None]
    decay = jnp.exp(jnp.where(causal, seg, -jnp.inf))
    cb = jnp.einsum('bclgn,bcsgn->bclsg', cm, bm)
    wts = cb[..., None] * decay * dt[:, :, None]
    y_diag = jnp.einsum('bclsgr,bcsgrp->bclgrp', wts, x)
    decay_states = jnp.exp(cs[:, :, -1:] - cs)
    xw = x * (decay_states * dt)[..., None]
    states = jnp.einsum('bcsgn,bcsgrp->bcgrpn', bm, xw)
    chunk_decay = jnp.exp(cs[:, :, -1])

    def step(carry, inp):
        st, dec = inp
        return carry * dec[..., None, None] + st, carry

    init = jnp.zeros((bsz, g_, r_, p_, n_), jnp.float32)
    _, prev = lax.scan(step, init, (jnp.moveaxis(states, 1, 0), jnp.moveaxis(chunk_decay, 1, 0)))
    prev = jnp.moveaxis(prev, 0, 1)
    y_off = jnp.einsum('bclgn,bcgrpn->bclgrp', cm, prev) * jnp.exp(cs)[..., None]
    y = y_diag + y_off + x * d_skip.astype(jnp.float32).reshape(g_, r_)[:, :, None]
    y = y.reshape(bsz, s, SSM_WIDTH).astype(z.dtype)
    yg = (y * jax.nn.silu(z)).reshape(bsz, s, g_, SSM_WIDTH // g_)
    return rmsnorm(yg, norm_g.reshape(g_, SSM_WIDTH // g_)).reshape(bsz, s, SSM_WIDTH)


def memory_cross_attention(h, mem_n, w_q, w_kv, w_o):
    bsz, s, d = h.shape
    m = mem_n.shape[1]
    q = (h @ w_q).reshape(bsz, s, XA_HEADS, XA_HEAD_DIM) * (XA_HEAD_DIM ** -0.5)
    k, v = jnp.split(mem_n @ w_kv, 2, axis=-1)
    k = k.reshape(bsz, m, XA_HEADS, XA_HEAD_DIM)
    v = v.reshape(bsz, m, XA_HEADS, XA_HEAD_DIM)
    scores = jnp.einsum('bshd,bmhd->bhsm', q, k).astype(jnp.float32)
    p = jax.nn.softmax(scores, axis=-1).astype(v.dtype)
    o = jnp.einsum('bhsm,bmhd->bshd', p, v).reshape(bsz, s, d)
    return o @ w_o


def hierarchical_moe(h, w_rg, b_rg, w_re, b_re, w_gu, w_dn):
    bsz, s, d = h.shape
    t = bsz * s
    xt = h.reshape(t, d)
    g_logits = (xt @ w_rg).astype(jnp.float32) + b_rg.astype(jnp.float32)
    grp = jnp.argmax(g_logits, axis=-1)
    p_grp = jnp.max(jax.nn.softmax(g_logits, axis=-1), axis=-1, keepdims=True)
    e_logits = ((xt @ w_re).astype(jnp.float32) + b_re.astype(jnp.float32)).reshape(t, N_EXPERT_GROUPS, EXPERTS_PER_GROUP)
    e_in = e_logits[jnp.arange(t), grp]
    top_v, top_i = lax.top_k(e_in, TOP_K_IN_GROUP)
    gate = p_grp * jax.nn.softmax(top_v, axis=-1)
    expert = grp[:, None] * EXPERTS_PER_GROUP + top_i
    n_assign = t * TOP_K_IN_GROUP
    flat_e = expert.reshape(n_assign)
    flat_tok = jnp.repeat(jnp.arange(t, dtype=jnp.int32), TOP_K_IN_GROUP)
    flat_w = gate.reshape(n_assign)
    order = jnp.argsort(flat_e)
    se, stok, sw = flat_e[order], flat_tok[order], flat_w[order]
    counts = jnp.bincount(flat_e, length=N_EXPERTS)
    starts = jnp.cumsum(counts) - counts
    padded = (counts + MOE_BLOCK - 1) // MOE_BLOCK * MOE_BLOCK
    pad_ends = jnp.cumsum(padded)
    pad_starts = pad_ends - padded
    dest = pad_starts[se] + (jnp.arange(n_assign) - starts[se])
    n_blocks = (n_assign + MOE_BLOCK - 1) // MOE_BLOCK + N_EXPERTS
    buf_len = n_blocks * MOE_BLOCK
    buf_tok = jnp.full((buf_len,), t, jnp.int32).at[dest].set(stok)
    buf_w = jnp.zeros((buf_len,), jnp.float32).at[dest].set(sw)
    block_e = jnp.minimum(jnp.searchsorted(pad_ends, jnp.arange(n_blocks) * MOE_BLOCK, side='right'), N_EXPERTS - 1)
    x_pad = jnp.concatenate([xt, jnp.zeros((1, d), xt.dtype)], axis=0)
    xb = x_pad[buf_tok].reshape(n_blocks, MOE_BLOCK, d)

    def expert_block(args):
        xblk, e = args
        gg, uu = jnp.split(xblk @ w_gu[e], 2, axis=-1)
        return (jax.nn.silu(gg) * uu) @ w_dn[e]

    yb = lax.map(expert_block, (xb, block_e)).reshape(buf_len, d)
    yb = yb * buf_w[:, None].astype(yb.dtype)
    out = jnp.zeros((t + 1, d), yb.dtype).at[buf_tok].add(yb)[:t]
    return out.reshape(bsz, s, d)


def setup_inputs(seed: int = 0) -> dict:
    key = jax.random.key(seed)
    ks = iter(jax.random.split(key, 48))

    def nrm(shape, scale):
        return jax.random.normal(next(ks), shape, jnp.float32) * scale

    def gain(shape):
        return 1.0 + nrm(shape, 0.05)

    L, D = DEPTH, D_MODEL
    inp = {}
    inp['x'] = nrm((BATCH, SEQ, D), 1.0)
    inp['mem'] = nrm((BATCH, MEM_LEN, D), 1.0)
    inp['norm_mix'] = gain((L, D))
    inp['w_in'] = nrm((L, D, IN_DIM), D ** -0.5)
    inp['conv_a_w'] = nrm((L, CONV_WIDTH, LRU_WIDTH), CONV_WIDTH ** -0.5)
    inp['conv_a_b'] = nrm((L, LRU_WIDTH), 0.02)
    inp['lru_gx_w'] = nrm((L, LRU_HEADS, LRU_HEAD_DIM, LRU_HEAD_DIM), LRU_HEAD_DIM ** -0.5)
    inp['lru_gx_b'] = nrm((L, LRU_HEADS, LRU_HEAD_DIM), 0.02)
    inp['lru_ga_w'] = nrm((L, LRU_HEADS, LRU_HEAD_DIM, LRU_HEAD_DIM), LRU_HEAD_DIM ** -0.5)
    inp['lru_ga_b'] = nrm((L, LRU_HEADS, LRU_HEAD_DIM), 0.02)
    a_c = jax.random.uniform(next(ks), (L, LRU_WIDTH), jnp.float32, 0.9, 0.999)
    p = a_c ** (1.0 / LRU_C)
    inp['lru_lambda'] = jnp.log(p) - jnp.log1p(-p)
    inp['sgu_norm'] = gain((L, SGU_WIDTH))
    inp['sgu_w'] = nrm((L, SGU_GROUPS, SGU_BLOCK, SGU_BLOCK), SGU_BLOCK ** -0.5)
    inp['sgu_b'] = gain((L, SGU_GROUPS, SGU_BLOCK))
    inp['conv_c_w'] = nrm((L, CONV_WIDTH, SSM_CONV_DIM), CONV_WIDTH ** -0.5)
    inp['conv_c_b'] = nrm((L, SSM_CONV_DIM), 0.02)
    dt0 = jnp.exp(jax.random.uniform(next(ks), (L, SSM_HEADS), jnp.float32, math.log(1e-3), math.log(1e-1)))
    inp['ssm_dt_bias'] = dt0 + jnp.log(-jnp.expm1(-dt0))
    inp['ssm_a_log'] = jnp.log(jax.random.uniform(next(ks), (L, SSM_HEADS), jnp.float32, 1.0, 16.0))
    inp['ssm_d'] = gain((L, SSM_HEADS))
    inp['ssm_norm'] = gain((L, SSM_WIDTH))
    inp['w_a_proj'] = nrm((L, LRU_WIDTH, D), LRU_WIDTH ** -0.5)
    inp['w_b_proj'] = nrm((L, SGU_WIDTH, D), SGU_WIDTH ** -0.5)
    inp['w_c_proj'] = nrm((L, SSM_WIDTH, D), SSM_WIDTH ** -0.5)
    inp['w_out'] = nrm((L, D, D), D ** -0.5)
    inp['norm_xa'] = gain((L, D))
    inp['norm_mem'] = gain((L, D))
    inp['w_q'] = nrm((L, D, D), D ** -0.5)
    inp['w_kv'] = nrm((L, D, 2 * D), D ** -0.5)
    inp['w_o'] = nrm((L, D, D), D ** -0.5)
    inp['norm_ffn'] = gain((L, D))
    inp['w_router_group'] = nrm((L, D, N_EXPERT_GROUPS), D ** -0.5)
    inp['b_router_group'] = nrm((L, N_EXPERT_GROUPS), 0.01)
    inp['w_router_expert'] = nrm((L, D, N_EXPERTS), D ** -0.5)
    inp['b_router_expert'] = nrm((L, N_EXPERTS), 0.01)
    inp['w_gate_up'] = nrm((L, N_EXPERTS, D, 2 * D_EXPERT), D ** -0.5)
    inp['w_down'] = nrm((L, N_EXPERTS, D_EXPERT, D), D_EXPERT ** -0.5)
    inp['norm_final'] = gain((D,))
    return inp


def reference(x, mem, norm_mix, w_in, conv_a_w, conv_a_b, lru_gx_w, lru_gx_b, lru_ga_w, lru_ga_b,
              lru_lambda, sgu_norm, sgu_w, sgu_b, conv_c_w, conv_c_b, ssm_dt_bias, ssm_a_log, ssm_d,
              ssm_norm, w_a_proj, w_b_proj, w_c_proj, w_out, norm_xa, norm_mem, w_q, w_kv, w_o,
              norm_ffn, w_router_group, b_router_group, w_router_expert, b_router_expert,
              w_gate_up, w_down, norm_final):
    bsz, s, d = x.shape
    for l in range(DEPTH):
        h = rmsnorm(x, norm_mix[l])
        proj = h @ w_in[l]
        xa, ya, uv, zc, xbc, dtc, gates = jnp.split(
            proj, [OFF_YA, OFF_UV, OFF_Z, OFF_XBC, OFF_DT, OFF_GATE], axis=-1)
        y_a = rglru_branch(xa, ya, conv_a_w[l], conv_a_b[l], lru_gx_w[l], lru_gx_b[l],
                           lru_ga_w[l], lru_ga_b[l], lru_lambda[l]) @ w_a_proj[l]
        y_b = sgu_branch(uv, sgu_norm[l], sgu_w[l], sgu_b[l]) @ w_b_proj[l]
        y_c = ssd_branch(zc, xbc, dtc, conv_c_w[l], conv_c_b[l], ssm_dt_bias[l], ssm_a_log[l],
                         ssm_d[l], ssm_norm[l]) @ w_c_proj[l]
        g = jax.nn.sigmoid(gates).reshape(bsz, s, N_BRANCHES, d)
        merged = g[:, :, 0] * y_a + g[:, :, 1] * y_b + g[:, :, 2] * y_c
        x = x + merged @ w_out[l]
        x = x + memory_cross_attention(rmsnorm(x, norm_xa[l]), rmsnorm(mem, norm_mem[l]),
                                       w_q[l], w_kv[l], w_o[l])
        x = x + hierarchical_moe(rmsnorm(x, norm_ffn[l]), w_router_group[l], b_router_group[l],
                                 w_router_expert[l], b_router_expert[l], w_gate_up[l], w_down[l])
    return rmsnorm(x, norm_final)
```

```python
import functools

import jax
import jax.numpy as jnp
from jax import lax
from jax.experimental import pallas as pl
from jax.experimental.pallas import tpu as pltpu

F32 = jnp.float32
BF16 = jnp.bfloat16

EPS = 1e-6
CONV_WIDTH = 4
CONV_PAD = 8
LRU_HEADS = 8
LRU_C = 8.0
SGU_GROUPS = 4
SGU_BLOCK = 128
SGU_CHUNK = 64
SSM_HEADS = 16
SSM_HEAD_DIM = 64
SSM_GROUPS = 4
SSM_STATE = 128
SSD_CHUNK = 128
XA_HEADS = 4
N_EXPERT_GROUPS = 4
EXPERTS_PER_GROUP = 8
N_EXPERTS = N_EXPERT_GROUPS * EXPERTS_PER_GROUP
MOE_BLOCK = 256
LANES = 128
SUBLANES = 8
VMEM_LIMIT_BYTES = 56 * 1024 * 1024


def _cparams(*sem):
    return pltpu.CompilerParams(dimension_semantics=sem, vmem_limit_bytes=VMEM_LIMIT_BYTES)


def _full(a):
    nd = a.ndim
    return pl.BlockSpec(a.shape, lambda *_: (0,) * nd)


def _bdot(a, b):
    return jnp.dot(a.astype(BF16), b.astype(BF16), preferred_element_type=F32)


def _bdot_nt(a, b):
    return lax.dot_general(a.astype(BF16), b.astype(BF16), (((1,), (1,)), ((), ())),
                           preferred_element_type=F32)


def _split3(v):
    hi = v.astype(BF16)
    r1 = v - hi.astype(F32)
    mid = r1.astype(BF16)
    lo = (r1 - mid.astype(F32)).astype(BF16)
    return hi, mid, lo


def _dot_exact_rhs(v, m01):
    hi, mid, lo = _split3(v)
    d = functools.partial(jnp.dot, preferred_element_type=F32)
    return d(hi, m01) + d(mid, m01) + d(lo, m01)


def _dot_exact_lhs(m01, v):
    hi, mid, lo = _split3(v)
    d = functools.partial(jnp.dot, preferred_element_type=F32)
    return d(m01, hi) + d(m01, mid) + d(m01, lo)


def _rms(x, g):
    return x * lax.rsqrt(jnp.mean(x * x, axis=-1, keepdims=True) + EPS) * g


def _sigmoid(x):
    return 1.0 / (1.0 + jnp.exp(-x))


def _silu(x):
    return x * _sigmoid(x)


def _softplus(x):
    return jnp.maximum(x, 0.0) + jnp.log1p(jnp.exp(-jnp.abs(x)))


def _gelu(x):
    return jax.nn.gelu(x)


def _conv4(buf, tile_rows, xin, cw, cb):
    buf[pl.ds(CONV_PAD, tile_rows), :] = xin
    acc = cb
    for k in range(CONV_WIDTH):
        acc = acc + cw[k:k + 1, :] * buf[pl.ds(CONV_PAD - (CONV_WIDTH - 1) + k, tile_rows), :]
    buf[pl.ds(0, CONV_PAD), :] = buf[pl.ds(tile_rows, CONV_PAD), :]
    return acc


def _lru_kernel(x_ref, nrm_ref, wxa_ref, wya_ref, wg_ref, cw_ref, cb_ref, gw_ref, gxb_ref, gab_ref,
                lam_ref, wp_ref, o_ref, xa_buf, a_buf, b_buf, h_buf, carry):
    ts, width = a_buf.shape
    j = pl.program_id(1)

    @pl.when(j == 0)
    def _():
        xa_buf[pl.ds(0, CONV_PAD), :] = jnp.zeros((CONV_PAD, width), F32)
        carry[...] = jnp.zeros_like(carry)

    h = _rms(x_ref[...], nrm_ref[...]).astype(BF16)
    xa = jnp.dot(h, wxa_ref[...], preferred_element_type=F32)
    xc = _conv4(xa_buf, ts, xa, cw_ref[...], cb_ref[...])

    pair = 2 * (width // LRU_HEADS)
    gxs, gas = [], []
    for p in range(LRU_HEADS // 2):
        g = _bdot(xc[:, p * pair:(p + 1) * pair], gw_ref[p])
        gxs.append(g[:, :pair])
        gas.append(g[:, pair:])
    gate_x = _sigmoid(jnp.concatenate(gxs, axis=1) + gxb_ref[...])
    gate_a = _sigmoid(jnp.concatenate(gas, axis=1) + gab_ref[...])
    log_a = (-LRU_C * gate_a) * _softplus(-lam_ref[...])
    a = jnp.exp(log_a)
    mult = jnp.sqrt(jnp.maximum(1.0 - a * a, 0.0))
    b = xc * gate_x * mult

    a3 = a.reshape(ts // SUBLANES, SUBLANES, width)
    b3 = b.reshape(ts // SUBLANES, SUBLANES, width)
    sub = lax.broadcasted_iota(jnp.int32, (1, SUBLANES, width), 1)
    for d in (1, 2, 4):
        keep = sub >= d
        a_sh = jnp.where(keep, pltpu.roll(a3, d, 1), 1.0)
        b_sh = jnp.where(keep, pltpu.roll(b3, d, 1), 0.0)
        b3 = a3 * b_sh + b3
        a3 = a3 * a_sh
    a_buf[...] = a3.reshape(ts, width)
    b_buf[...] = b3.reshape(ts, width)

    def group(gi, hc):
        rows = pl.ds(pl.multiple_of(gi * SUBLANES, SUBLANES), SUBLANES)
        hg = b_buf[rows, :] + a_buf[rows, :] * hc
        h_buf[rows, :] = hg
        return jnp.broadcast_to(hg[SUBLANES - 1:SUBLANES, :], (SUBLANES, width))

    carry[...] = lax.fori_loop(0, ts // SUBLANES, group, carry[...])

    ya = jnp.dot(h, wya_ref[...], preferred_element_type=F32)
    out = h_buf[...] * _gelu(ya)
    y = _bdot(out, wp_ref[...])
    gate = _sigmoid(jnp.dot(h, wg_ref[...], preferred_element_type=F32))
    o_ref[...] = (gate * y).astype(o_ref.dtype)


def _lru_branch(x2d, bsz, seq, nrm, wxa, wya, wg, cw, cb, gw, gxb, gab, lam, wp):
    t, d = x2d.shape
    width = wxa.shape[1]
    ts = min(512, seq)
    nj = seq // ts
    row = lambda b, j: (b * nj + j, 0)
    consts = (nrm, wxa, wya, wg, cw, cb, gw, gxb, gab, lam, wp)
    return pl.pallas_call(
        _lru_kernel,
        grid=(bsz, nj),
        in_specs=[pl.BlockSpec((ts, d), row)] + [_full(c) for c in consts],
        out_specs=pl.BlockSpec((ts, d), row),
        out_shape=jax.ShapeDtypeStruct((t, d), BF16),
        scratch_shapes=[pltpu.VMEM((ts + CONV_PAD, width), F32), pltpu.VMEM((ts, width), F32),
                        pltpu.VMEM((ts, width), F32), pltpu.VMEM((ts, width), F32),
                        pltpu.VMEM((SUBLANES, width), F32)],
        compiler_params=_cparams("arbitrary", "arbitrary"),
        name="lru_branch",
    )(x2d, *consts)


def _sgu_kernel(x_ref, nrm_ref, wu_ref, wv_ref, wg_ref, ng_ref, ws_ref, bias_ref, wp_ref, o_ref, mix_buf):
    ts, width = mix_buf.shape
    gdim = width // SGU_GROUPS
    h = _rms(x_ref[...], nrm_ref[...]).astype(BF16)
    u = _gelu(jnp.dot(h, wu_ref[...], preferred_element_type=F32))
    v = _gelu(jnp.dot(h, wv_ref[...], preferred_element_type=F32))
    vc = v - jnp.mean(v, axis=-1, keepdims=True)
    v = (vc * lax.rsqrt(jnp.mean(vc * vc, axis=-1, keepdims=True) + EPS) * ng_ref[...]).astype(BF16)

    ti = lax.broadcasted_iota(jnp.int32, (SGU_BLOCK, SGU_BLOCK), 0) // SGU_CHUNK
    si = lax.broadcasted_iota(jnp.int32, (SGU_BLOCK, SGU_BLOCK), 1) // SGU_CHUNK
    mask = si <= ti
    wm = [jnp.where(mask, ws_ref[g], 0.0).astype(BF16) for g in range(SGU_GROUPS)]
    for blk in range(ts // SGU_BLOCK):
        rows = slice(blk * SGU_BLOCK, (blk + 1) * SGU_BLOCK)
        for g in range(SGU_GROUPS):
            cols = slice(g * gdim, (g + 1) * gdim)
            mix_buf[rows, cols] = jnp.dot(wm[g], v[rows, cols], preferred_element_type=F32) + bias_ref[:, cols]
    out = u * mix_buf[...]
    y = _bdot(out, wp_ref[...])
    gate = _sigmoid(jnp.dot(h, wg_ref[...], preferred_element_type=F32))
    o_ref[...] = (gate * y).astype(o_ref.dtype)


def _sgu_branch(x2d, seq, nrm, wu, wv, wg, ng, ws, bias, wp):
    t, d = x2d.shape
    width = wu.shape[1]
    ts = min(512, seq)
    consts = (nrm, wu, wv, wg, ng, ws, bias, wp)
    return pl.pallas_call(
        _sgu_kernel,
        grid=(t // ts,),
        in_specs=[pl.BlockSpec((ts, d), lambda i: (i, 0))] + [_full(c) for c in consts],
        out_specs=pl.BlockSpec((ts, d), lambda i: (i, 0)),
        out_shape=jax.ShapeDtypeStruct((t, d), BF16),
        scratch_shapes=[pltpu.VMEM((ts, width), F32)],
        compiler_params=_cparams("arbitrary"),
        name="sgu_branch",
    )(x2d, *consts)


def _ssd_kernel(x_ref, nrm_ref, wz_ref, wxbc_ref, wdt_ref, wg_ref, cw_ref, cb_ref, dtb_ref, alog_ref,
                dskip_ref, expand_ref, ng_ref, wp_ref, o_ref, xbc_buf, y_buf, state):
    ts, width = y_buf.shape
    lc = SSD_CHUNK
    n_state = SSM_STATE
    gw = width // SSM_GROUPS
    hpg = SSM_HEADS // SSM_GROUPS
    p_dim = SSM_HEAD_DIM
    j = pl.program_id(1)

    @pl.when(j == 0)
    def _():
        xbc_buf[pl.ds(0, CONV_PAD), :] = jnp.zeros((CONV_PAD, xbc_buf.shape[1]), F32)
        state[...] = jnp.zeros_like(state)

    h = _rms(x_ref[...], nrm_ref[...]).astype(BF16)
    xbc = jnp.dot(h, wxbc_ref[...], preferred_element_type=F32)
    xc = _silu(_conv4(xbc_buf, ts, xbc, cw_ref[...], cb_ref[...]))
    dt = _softplus(jnp.dot(h, wdt_ref[...], preferred_element_type=F32) + dtb_ref[...])
    lane = lax.broadcasted_iota(jnp.int32, (1, LANES), 1)
    a_row = jnp.where(lane < SSM_HEADS, -jnp.exp(alog_ref[...]), 0.0)
    da = dt * a_row

    li = lax.broadcasted_iota(jnp.int32, (lc, lc), 0)
    si = lax.broadcasted_iota(jnp.int32, (lc, lc), 1)
    causal = li >= si
    tri = causal.astype(BF16)
    plane = lax.broadcasted_iota(jnp.int32, (lc, 2 * p_dim), 1)
    expand = expand_ref[...]

    for c in range(ts // lc):
        rows = slice(c * lc, (c + 1) * lc)
        dt_c = dt[rows]
        cs = _dot_exact_lhs(tri, da[rows])
        cs_t = cs.T
        dt_t = dt_c.T
        cs_last = cs[lc - 1:lc, :]
        wst_e = _bdot(jnp.exp(cs_last - cs) * dt_c, expand)
        ecs_e = _bdot(jnp.exp(cs), expand)
        cd_e = _dot_exact_rhs(jnp.broadcast_to(jnp.exp(cs_last), (SUBLANES, LANES)), expand)[0:1, :]
        xs_c = xc[rows, :width]
        xw = (xs_c * wst_e).astype(BF16)
        xs_b = xs_c.astype(BF16)
        for g in range(SSM_GROUPS):
            bm = xc[rows, width + g * n_state: width + (g + 1) * n_state]
            cm = xc[rows, width + (SSM_GROUPS + g) * n_state: width + (SSM_GROUPS + g + 1) * n_state].astype(BF16)
            cb = _bdot_nt(cm, bm)
            st = state[g]
            y_g = jnp.dot(cm, st.astype(BF16), preferred_element_type=F32) * ecs_e[:, g * gw:(g + 1) * gw]
            yds = []
            for pr in range(hpg // 2):
                ws = []
                for r in (2 * pr, 2 * pr + 1):
                    hh = g * hpg + r
                    seg = cs[:, hh:hh + 1] - cs_t[hh:hh + 1, :]
                    ws.append((cb * jnp.exp(jnp.where(causal, seg, -jnp.inf)) * dt_t[hh:hh + 1, :]).astype(BF16))
                xp = xs_b[:, g * gw + pr * 2 * p_dim: g * gw + (pr + 1) * 2 * p_dim]
                zero = jnp.zeros_like(xp)
                rhs = jnp.concatenate([jnp.where(plane < p_dim, xp, zero), jnp.where(plane >= p_dim, xp, zero)], axis=0)
                yds.append(jnp.dot(jnp.concatenate(ws, axis=1), rhs, preferred_element_type=F32))
            y_buf[rows, g * gw:(g + 1) * gw] = y_g + jnp.concatenate(yds, axis=1)
            new = jnp.dot(bm.T.astype(BF16), xw[:, g * gw:(g + 1) * gw], preferred_element_type=F32)
            state[g] = st * cd_e[:, g * gw:(g + 1) * gw] + new

    z = jnp.dot(h, wz_ref[...], preferred_element_type=F32)
    y = y_buf[...] + xc[:, :width] * dskip_ref[...]
    yg = y * _silu(z)
    ng = ng_ref[...]
    parts = []
    for g in range(SSM_GROUPS):
        seg = yg[:, g * gw:(g + 1) * gw]
        parts.append(seg * lax.rsqrt(jnp.mean(seg * seg, axis=-1, keepdims=True) + EPS) * ng[:, g * gw:(g + 1) * gw])
    yn = jnp.concatenate(parts, axis=1)
    out = _bdot(yn, wp_ref[...])
    gate = _sigmoid(jnp.dot(h, wg_ref[...], preferred_element_type=F32))
    o_ref[...] = (gate * out).astype(o_ref.dtype)


def _ssd_branch(x2d, bsz, seq, nrm, wz, wxbc, wdt, wg, cw, cb, dtb, alog, dskip, expand, ng, wp):
    t, d = x2d.shape
    width = wz.shape[1]
    ts = min(256, seq)
    nj = seq // ts
    row = lambda b, j: (b * nj + j, 0)
    consts = (nrm, wz, wxbc, wdt, wg, cw, cb, dtb, alog, dskip, expand, ng, wp)
    return pl.pallas_call(
        _ssd_kernel,
        grid=(bsz, nj),
        in_specs=[pl.BlockSpec((ts, d), row)] + [_full(c) for c in consts],
        out_specs=pl.BlockSpec((ts, d), row),
        out_shape=jax.ShapeDtypeStruct((t, d), BF16),
        scratch_shapes=[pltpu.VMEM((ts + CONV_PAD, wxbc.shape[1]), F32), pltpu.VMEM((ts, width), F32),
                        pltpu.VMEM((SSM_GROUPS, SSM_STATE, width // SSM_GROUPS), F32)],
        compiler_params=_cparams("arbitrary", "arbitrary"),
        name="ssd_branch",
    )(x2d, *consts)


def _kv_kernel(m_ref, nrm_ref, w_ref, k_ref, v_ref):
    d = k_ref.shape[1]
    h = _rms(m_ref[...], nrm_ref[...]).astype(BF16)
    kv = jnp.dot(h, w_ref[...], preferred_element_type=F32)
    k_ref[...] = kv[:, :d].astype(k_ref.dtype)
    v_ref[...] = kv[:, d:].astype(v_ref.dtype)


def _kv_proj(mem2d, nrm, wkv):
    t, d = mem2d.shape
    ts = min(512, t)
    return pl.pallas_call(
        _kv_kernel,
        grid=(t // ts,),
        in_specs=[pl.BlockSpec((ts, d), lambda i: (i, 0)), _full(nrm), _full(wkv)],
        out_specs=[pl.BlockSpec((ts, d), lambda i: (i, 0))] * 2,
        out_shape=[jax.ShapeDtypeStruct((t, d), BF16)] * 2,
        compiler_params=_cparams("arbitrary"),
        name="kv_proj",
    )(mem2d, nrm, wkv)


def _attn_kernel(x_ref, ma_ref, mb_ref, mc_ref, k_ref, v_ref, wout_ref, nxa_ref, wq_ref, wo_ref, nffn_ref,
                 wr_ref, br_ref, x2_ref, hf_ref, lg_ref):
    d = x_ref.shape[1]
    hd = d // XA_HEADS
    merged = ma_ref[...].astype(F32) + mb_ref[...].astype(F32) + mc_ref[...].astype(F32)
    x1 = x_ref[...] + _bdot(merged, wout_ref[...])
    q = (_bdot(_rms(x1, nxa_ref[...]), wq_ref[...]) * (hd ** -0.5)).astype(BF16)
    outs = []
    for hh in range(XA_HEADS):
        cols = slice(hh * hd, (hh + 1) * hd)
        s = _bdot_nt(q[:, cols], k_ref[:, cols])
        s = jnp.exp(s - jnp.max(s, axis=-1, keepdims=True))
        p = s / jnp.sum(s, axis=-1, keepdims=True)
        outs.append(_bdot(p, v_ref[:, cols]))
    x2 = x1 + _bdot(jnp.concatenate(outs, axis=1), wo_ref[...])
    x2_ref[...] = x2
    hf = _rms(x2, nffn_ref[...])
    hf_ref[...] = hf
    lg_ref[...] = _bdot_nt(wr_ref[...], hf) + br_ref[...]


def _attn_block(x2d, ma, mb, mc, k2d, v2d, bsz, seq, mem_len, wout, nxa, wq, wo, nffn, wr_t, br_col):
    t, d = x2d.shape
    ts = min(512, seq)
    nj = seq // ts
    row = lambda i: (i, 0)
    memrow = lambda i: (i // nj, 0)
    consts = (wout, nxa, wq, wo, nffn, wr_t, br_col)
    return pl.pallas_call(
        _attn_kernel,
        grid=(t // ts,),
        in_specs=[pl.BlockSpec((ts, d), row)] * 4 + [pl.BlockSpec((mem_len, d), memrow)] * 2
                 + [_full(c) for c in consts],
        out_specs=[pl.BlockSpec((ts, d), row), pl.BlockSpec((ts, d), row), pl.BlockSpec((LANES, ts), lambda i: (0, i))],
        out_shape=[jax.ShapeDtypeStruct((t, d), F32), jax.ShapeDtypeStruct((t, d), F32),
                   jax.ShapeDtypeStruct((LANES, t), F32)],
        compiler_params=_cparams("arbitrary"),
        name="attn_block",
    )(x2d, ma, mb, mc, k2d, v2d, *consts)


def _first_max(rows):
    best = rows[0]
    for r in rows[1:]:
        best = jnp.maximum(best, r)
    idx = jnp.full(best.shape, len(rows) - 1, jnp.int32)
    for i in range(len(rows) - 2, -1, -1):
        idx = jnp.where(rows[i] >= best, i, idx)
    return best, idx


def _route_kernel(lg_ref, ids_ref, wts_ref, cnt_ref, carry):
    ts = lg_ref.shape[1]
    i = pl.program_id(0)

    @pl.when(i == 0)
    def _():
        carry[...] = jnp.zeros_like(carry)

    lg = lg_ref[...]
    grows = [lg[g:g + 1, :] for g in range(N_EXPERT_GROUPS)]
    gmax, grp = _first_max(grows)
    denom = grows[0] * 0.0
    for r in grows:
        denom = denom + jnp.exp(r - gmax)
    p_grp = 1.0 / denom

    erows = []
    for e in range(EXPERTS_PER_GROUP):
        r_last = N_EXPERT_GROUPS + (N_EXPERT_GROUPS - 1) * EXPERTS_PER_GROUP + e
        sel = lg[r_last:r_last + 1, :]
        for g in range(N_EXPERT_GROUPS - 2, -1, -1):
            r0 = N_EXPERT_GROUPS + g * EXPERTS_PER_GROUP + e
            sel = jnp.where(grp == g, lg[r0:r0 + 1, :], sel)
        erows.append(sel)
    v1, i1 = _first_max(erows)
    rest = [jnp.where(i1 == e, -jnp.inf, erows[e]) for e in range(EXPERTS_PER_GROUP)]
    v2, i2 = _first_max(rest)
    tt = jnp.exp(v2 - v1)
    w1 = p_grp / (1.0 + tt)
    w2 = p_grp * tt / (1.0 + tt)
    e1 = grp * EXPERTS_PER_GROUP + i1
    e2 = grp * EXPERTS_PER_GROUP + i2

    sub = lax.broadcasted_iota(jnp.int32, (N_EXPERTS, ts), 0)
    oh1 = sub == e1
    oh2 = sub == e2
    oh = jnp.where(oh1 | oh2, 1.0, 0.0)
    ti = lax.broadcasted_iota(jnp.int32, (ts, ts), 0)
    tj = lax.broadcasted_iota(jnp.int32, (ts, ts), 1)
    before = (ti < tj).astype(BF16)
    base = carry[...]
    rank = jnp.dot(oh.astype(BF16), before, preferred_element_type=F32) \
        + jnp.concatenate([base] * (ts // LANES), axis=1)
    rank1 = jnp.sum(jnp.where(oh1, rank, 0.0), axis=0, keepdims=True)
    rank2 = jnp.sum(jnp.where(oh2, rank, 0.0), axis=0, keepdims=True)
    carry[...] = base + jnp.sum(oh, axis=1, keepdims=True)

    zi = jnp.zeros((SUBLANES - 4, ts), jnp.int32)
    ids_ref[...] = jnp.concatenate([e1, e2, rank1.astype(jnp.int32), rank2.astype(jnp.int32), zi], axis=0)
    wts_ref[...] = jnp.concatenate([w1, w2, jnp.zeros((SUBLANES - 2, ts), F32)], axis=0)
    cnt_ref[...] = carry[...].astype(jnp.int32)


def _route(logits_t):
    t = logits_t.shape[1]
    ts = min(512, t)
    return pl.pallas_call(
        _route_kernel,
        grid=(t // ts,),
        in_specs=[pl.BlockSpec((LANES, ts), lambda i: (0, i))],
        out_specs=[pl.BlockSpec((SUBLANES, ts), lambda i: (0, i)), pl.BlockSpec((SUBLANES, ts), lambda i: (0, i)),
                   pl.BlockSpec((N_EXPERTS, LANES), lambda i: (0, 0))],
        out_shape=[jax.ShapeDtypeStruct((SUBLANES, t), jnp.int32), jax.ShapeDtypeStruct((SUBLANES, t), F32),
                   jax.ShapeDtypeStruct((N_EXPERTS, LANES), jnp.int32)],
        scratch_shapes=[pltpu.VMEM((N_EXPERTS, LANES), F32)],
        compiler_params=_cparams("arbitrary"),
        name="moe_route",
    )(logits_t)


def _dest_kernel(ids_ref, ps_ref, dst_ref):
    ts = ids_ref.shape[1]
    ids = ids_ref[...]
    sub = lax.broadcasted_iota(jnp.int32, (N_EXPERTS, ts), 0)
    ps = jnp.concatenate([ps_ref[...]] * (ts // LANES), axis=1)
    d1 = jnp.sum(jnp.where(sub == ids[0:1, :], ps, 0), axis=0, keepdims=True) + ids[2:3, :]
    d2 = jnp.sum(jnp.where(sub == ids[1:2, :], ps, 0), axis=0, keepdims=True) + ids[3:4, :]
    dst_ref[...] = jnp.concatenate([d1, d2, jnp.zeros((SUBLANES - 2, ts), jnp.int32)], axis=0)


def _dest(ids, pad_starts_rep):
    t = ids.shape[1]
    ts = min(2048, t)
    return pl.pallas_call(
        _dest_kernel,
        grid=(t // ts,),
        in_specs=[pl.BlockSpec((SUBLANES, ts), lambda i: (0, i)), _full(pad_starts_rep)],
        out_specs=pl.BlockSpec((SUBLANES, ts), lambda i: (0, i)),
        out_shape=jax.ShapeDtypeStruct((SUBLANES, t), jnp.int32),
        compiler_params=_cparams("arbitrary"),
        name="moe_dest",
    )(ids, pad_starts_rep)


def _row_copy(src, src_row, dst, dst_row, sem):
    return pltpu.make_async_copy(src.at[pl.ds(src_row, 1), :], dst.at[pl.ds(dst_row, 1), :], sem)


def _dispatch_kernel(d1_ref, d2_ref, hf_ref, init_ref, xs_ref, sem):
    del init_ref
    ts = hf_ref.shape[0]
    base = pl.program_id(0) * ts

    def issue(r, c):
        _row_copy(hf_ref, r, xs_ref, d1_ref[base + r], sem).start()
        _row_copy(hf_ref, r, xs_ref, d2_ref[base + r], sem).start()
        return c

    lax.fori_loop(0, ts, issue, 0)
    for _ in range(2):
        pltpu.make_async_copy(hf_ref, xs_ref.at[pl.ds(0, ts), :], sem).wait()


def _dispatch(d1, d2, hf, n_slots):
    t, d = hf.shape
    ts = min(256, t)
    grid_spec = pltpu.PrefetchScalarGridSpec(
        num_scalar_prefetch=2,
        grid=(t // ts,),
        in_specs=[pl.BlockSpec((ts, d), lambda i, a, b: (i, 0)), pl.BlockSpec(memory_space=pl.ANY)],
        out_specs=pl.BlockSpec(memory_space=pl.ANY),
        scratch_shapes=[pltpu.SemaphoreType.DMA],
    )
    return pl.pallas_call(
        _dispatch_kernel,
        grid_spec=grid_spec,
        out_shape=jax.ShapeDtypeStruct((n_slots, d), F32),
        input_output_aliases={3: 0},
        compiler_params=_cparams("arbitrary"),
        name="moe_dispatch",
    )(d1, d2, hf, jnp.zeros((n_slots, d), F32))


def _expert_kernel(be_ref, nu_ref, xs_ref, wgu_ref, wdn_ref, y_ref):
    b = pl.program_id(0)
    de = wdn_ref.shape[1]

    @pl.when(b < nu_ref[0])
    def _():
        gu = _bdot(xs_ref[...], wgu_ref[0])
        act = _silu(gu[:, :de]) * gu[:, de:]
        y_ref[...] = _bdot(act, wdn_ref[0])

    @pl.when(b >= nu_ref[0])
    def _():
        y_ref[...] = jnp.zeros_like(y_ref)


def _experts(block_e, n_used, xs, wgu, wdn):
    n_slots, d = xs.shape
    n_blocks = n_slots // MOE_BLOCK

    def blk(b, be, nu):
        return jnp.minimum(b, nu[0] - 1)

    grid_spec = pltpu.PrefetchScalarGridSpec(
        num_scalar_prefetch=2,
        grid=(n_blocks,),
        in_specs=[pl.BlockSpec((MOE_BLOCK, d), lambda b, be, nu: (blk(b, be, nu), 0)),
                  pl.BlockSpec((1,) + wgu.shape[1:], lambda b, be, nu: (be[blk(b, be, nu)], 0, 0)),
                  pl.BlockSpec((1,) + wdn.shape[1:], lambda b, be, nu: (be[blk(b, be, nu)], 0, 0))],
        out_specs=pl.BlockSpec((MOE_BLOCK, d), lambda b, be, nu: (b, 0)),
    )
    return pl.pallas_call(
        _expert_kernel,
        grid_spec=grid_spec,
        out_shape=jax.ShapeDtypeStruct((n_slots, d), F32),
        compiler_params=_cparams("arbitrary"),
        name="moe_experts",
    )(block_e, n_used, xs, wgu, wdn)


def _combine_kernel(d1_ref, d2_ref, x_ref, w1_ref, w2_ref, nf_ref, yb_ref, o_ref, buf1, buf2, sem, *, final_norm):
    ts = x_ref.shape[0]
    base = pl.program_id(0) * ts

    def issue(r, c):
        _row_copy(yb_ref, d1_ref[base + r], buf1, r, sem).start()
        _row_copy(yb_ref, d2_ref[base + r], buf2, r, sem).start()
        return c

    lax.fori_loop(0, ts, issue, 0)
    pltpu.make_async_copy(yb_ref.at[pl.ds(0, ts), :], buf1, sem).wait()
    pltpu.make_async_copy(yb_ref.at[pl.ds(0, ts), :], buf2, sem).wait()
    out = x_ref[...] + w1_ref[...] * buf1[...] + w2_ref[...] * buf2[...]
    if final_norm:
        out = _rms(out, nf_ref[...])
    o_ref[...] = out


def _combine(d1, d2, x2, w1c, w2c, nf, yb, final_norm):
    t, d = x2.shape
    ts = min(256, t)
    row = lambda i, a, b: (i, 0)
    grid_spec = pltpu.PrefetchScalarGridSpec(
        num_scalar_prefetch=2,
        grid=(t // ts,),
        in_specs=[pl.BlockSpec((ts, d), row), pl.BlockSpec((ts, 1), row), pl.BlockSpec((ts, 1), row),
                  pl.BlockSpec((1, d), lambda i, a, b: (0, 0)), pl.BlockSpec(memory_space=pl.ANY)],
        out_specs=pl.BlockSpec((ts, d), row),
        scratch_shapes=[pltpu.VMEM((ts, d), F32), pltpu.VMEM((ts, d), F32), pltpu.SemaphoreType.DMA],
    )
    return pl.pallas_call(
        functools.partial(_combine_kernel, final_norm=final_norm),
        grid_spec=grid_spec,
        out_shape=jax.ShapeDtypeStruct((t, d), F32),
        compiler_params=_cparams("arbitrary"),
        name="moe_combine",
    )(d1, d2, x2, w1c, w2c, nf, yb)


def _moe(x2, hf, logits_t, wgu, wdn, nf, final_norm):
    t, d = x2.shape
    ids, wts, counts = _route(logits_t)
    counts = counts[:, 0]
    padded = (counts + MOE_BLOCK - 1) // MOE_BLOCK * MOE_BLOCK
    pad_ends = jnp.cumsum(padded)
    pad_starts = pad_ends - padded
    n_blocks = (2 * t + MOE_BLOCK - 1) // MOE_BLOCK + N_EXPERTS
    block_e = jnp.minimum(jnp.searchsorted(pad_ends, jnp.arange(n_blocks, dtype=jnp.int32) * MOE_BLOCK, side='right'),
                          N_EXPERTS - 1).astype(jnp.int32)
    n_used = (pad_ends[-1:] // MOE_BLOCK).astype(jnp.int32)
    dst = _dest(ids, jnp.broadcast_to(pad_starts[:, None].astype(jnp.int32), (N_EXPERTS, LANES)))
    d1, d2 = dst[0], dst[1]
    xs = _dispatch(d1, d2, hf, n_blocks * MOE_BLOCK)
    yb = _experts(block_e, n_used, xs, wgu, wdn)
    return _combine(d1, d2, x2, wts[0].reshape(t, 1), wts[1].reshape(t, 1), nf, yb, final_norm)


def _pair_blockdiag(w):
    z = jnp.zeros_like(w[0::2])
    top = jnp.concatenate([w[0::2], z], axis=2)
    bot = jnp.concatenate([z, w[1::2]], axis=2)
    return jnp.concatenate([top, bot], axis=1)


def kernel(x, mem, norm_mix, w_in, conv_a_w, conv_a_b, lru_gx_w, lru_gx_b, lru_ga_w, lru_ga_b, lru_lambda, sgu_norm, sgu_w, sgu_b, conv_c_w, conv_c_b, ssm_dt_bias, ssm_a_log, ssm_d, ssm_norm, w_a_proj, w_b_proj, w_c_proj, w_out, norm_xa, norm_mem, w_q, w_kv, w_o, norm_ffn, w_router_group, b_router_group, w_router_expert, b_router_expert, w_gate_up, w_down, norm_final):
    bsz, seq, d = x.shape
    mem_len = mem.shape[1]
    depth = w_in.shape[0]
    lru_w = conv_a_w.shape[2]
    sgu_wd = sgu_norm.shape[1]
    ssm_w = ssm_norm.shape[1]
    conv_dim = conv_c_w.shape[2]
    off_ya = lru_w
    off_uv = 2 * lru_w
    off_z = off_uv + 2 * sgu_wd
    off_xbc = off_z + ssm_w
    off_dt = off_xbc + conv_dim
    off_gate = off_dt + SSM_HEADS
    t = bsz * seq
    row = lambda v: v.reshape(1, -1).astype(F32)

    expand = (jnp.arange(LANES)[:, None] == (jnp.arange(ssm_w)[None, :] // SSM_HEAD_DIM)).astype(BF16)

    xcur = x.reshape(t, d)
    mem2d = mem.reshape(bsz * mem_len, d)
    for l in range(depth):
        win = w_in[l].astype(BF16)
        nrm = row(norm_mix[l])
        gates = win[:, off_gate:]
        gw = jnp.concatenate([_pair_blockdiag(lru_gx_w[l]), _pair_blockdiag(lru_ga_w[l])], axis=2).astype(BF16)
        m_a = _lru_branch(xcur, bsz, seq, nrm, win[:, :off_ya], win[:, off_ya:off_uv], gates[:, :d],
                          conv_a_w[l], row(conv_a_b[l]), gw, row(lru_gx_b[l]), row(lru_ga_b[l]),
                          row(lru_lambda[l]), w_a_proj[l].astype(BF16))
        sgu_bias = jnp.repeat(sgu_b[l].T, sgu_wd // SGU_GROUPS, axis=1)
        m_b = _sgu_branch(xcur, seq, nrm, win[:, off_uv:off_uv + sgu_wd], win[:, off_uv + sgu_wd:off_z],
                          gates[:, d:2 * d], row(sgu_norm[l]), sgu_w[l], sgu_bias, w_b_proj[l].astype(BF16))
        pad16 = lambda v: jnp.pad(v.reshape(1, -1), ((0, 0), (0, LANES - SSM_HEADS)))
        wdt = jnp.pad(win[:, off_dt:off_gate], ((0, 0), (0, LANES - SSM_HEADS)))
        m_c = _ssd_branch(xcur, bsz, seq, nrm, win[:, off_z:off_xbc], win[:, off_xbc:off_dt], wdt,
                          gates[:, 2 * d:], conv_c_w[l], row(conv_c_b[l]), pad16(ssm_dt_bias[l]),
                          pad16(ssm_a_log[l]), row(jnp.repeat(ssm_d[l], SSM_HEAD_DIM)), expand,
                          row(ssm_norm[l]), w_c_proj[l].astype(BF16))
        k2d, v2d = _kv_proj(mem2d, row(norm_mem[l]), w_kv[l].astype(BF16))
        n_r = N_EXPERT_GROUPS + N_EXPERTS
        wr_t = jnp.pad(jnp.concatenate([w_router_group[l], w_router_expert[l]], axis=1).T,
                       ((0, LANES - n_r), (0, 0))).astype(BF16)
        br_col = jnp.pad(jnp.concatenate([b_router_group[l], b_router_expert[l]]), (0, LANES - n_r)).reshape(LANES, 1)
        x2, hf, logits_t = _attn_block(xcur, m_a, m_b, m_c, k2d, v2d, bsz, seq, mem_len, w_out[l].astype(BF16),
                                       row(norm_xa[l]), w_q[l].astype(BF16), w_o[l].astype(BF16),
                                       row(norm_ffn[l]), wr_t, br_col)
        xcur = _moe(x2, hf, logits_t, w_gate_up[l].astype(BF16), w_down[l].astype(BF16), row(norm_final),
                    final_norm=(l == depth - 1))
    return xcur.reshape(bsz, seq, d)
```

```python
import functools
from typing import NamedTuple

import jax
import jax.numpy as jnp
from jax import lax
from jax.experimental import pallas as pl
from jax.experimental.pallas import tpu as pltpu

F32 = jnp.float32
BF16 = jnp.bfloat16

EPS = 1e-6
TINY = float(jnp.finfo(jnp.float32).tiny)
CONV_WIDTH = 4
CONV_PAD = 8
LRU_HEADS = 8
LRU_C = 8.0
SGU_GROUPS = 4
SGU_BLOCK = 128
SGU_CHUNK = 64
SSM_HEADS = 16
SSM_HEAD_DIM = 64
SSM_GROUPS = 4
SSM_STATE = 128
SSD_CHUNK = 128
XA_HEADS = 4
N_EXPERT_GROUPS = 4
EXPERTS_PER_GROUP = 8
N_EXPERTS = N_EXPERT_GROUPS * EXPERTS_PER_GROUP
MOE_BLOCK = 256
DISPATCH_SLOTS = 3
LANES = 128
SUBLANES = 8
VMEM_LIMIT_BYTES = 56 * 1024 * 1024


def _cparams(*sem):
    return pltpu.CompilerParams(dimension_semantics=sem, vmem_limit_bytes=VMEM_LIMIT_BYTES)


def _full(a):
    nd = a.ndim
    return pl.BlockSpec(a.shape, lambda *_: (0,) * nd)


class _Cols(NamedTuple):
    arr: jax.Array
    off: int
    width: int


def _spec(c):
    if isinstance(c, _Cols):
        blk, rem = divmod(c.off, c.width)
        assert rem == 0
        return pl.BlockSpec((c.arr.shape[0], c.width), lambda *_: (0, blk))
    return _full(c)


def _arr(c):
    return c.arr if isinstance(c, _Cols) else c


def _bdot(a, b):
    return jnp.dot(a.astype(BF16), b.astype(BF16), preferred_element_type=F32)


def _bdot_nt(a, b):
    return lax.dot_general(a.astype(BF16), b.astype(BF16), (((1,), (1,)), ((), ())),
                           preferred_element_type=F32)


def _split3(v):
    hi = v.astype(BF16)
    r1 = v - hi.astype(F32)
    mid = r1.astype(BF16)
    lo = (r1 - mid.astype(F32)).astype(BF16)
    return hi, mid, lo


def _dot_exact_rhs(v, m01):
    hi, mid, lo = _split3(v)
    d = functools.partial(jnp.dot, preferred_element_type=F32)
    return d(hi, m01) + d(mid, m01) + d(lo, m01)


def _dot_exact_lhs(m01, v):
    hi, mid, lo = _split3(v)
    d = functools.partial(jnp.dot, preferred_element_type=F32)
    return d(m01, hi) + d(m01, mid) + d(m01, lo)


def _rms(x, g):
    return x * lax.rsqrt(jnp.mean(x * x, axis=-1, keepdims=True) + EPS) * g


def _sigmoid(x):
    return 1.0 / (1.0 + jnp.exp(-x))


def _silu(x):
    return x * _sigmoid(x)


def _softplus(x):
    return jnp.maximum(x, 0.0) + jnp.log1p(jnp.exp(-jnp.abs(x)))


def _gelu(x):
    return jax.nn.gelu(x)


def _conv4(buf, tile_rows, xin, cw, cb):
    buf[pl.ds(CONV_PAD, tile_rows), :] = xin
    acc = cb
    for k in range(CONV_WIDTH):
        acc = acc + cw[k:k + 1, :] * buf[pl.ds(CONV_PAD - (CONV_WIDTH - 1) + k, tile_rows), :]
    buf[pl.ds(0, CONV_PAD), :] = buf[pl.ds(tile_rows, CONV_PAD), :]
    return acc


def _lru_kernel(x_ref, nrm_ref, wxa_ref, wya_ref, wg_ref, cw_ref, cb_ref, gw_ref, gxb_ref, gab_ref,
                lam_ref, wp_ref, o_ref, xa_buf, a_buf, b_buf, h_buf, carry):
    ts, width = a_buf.shape
    j = pl.program_id(1)

    @pl.when(j == 0)
    def _():
        xa_buf[pl.ds(0, CONV_PAD), :] = jnp.zeros((CONV_PAD, width), F32)
        carry[...] = jnp.zeros_like(carry)

    h = _rms(x_ref[...], nrm_ref[...]).astype(BF16)
    xa = jnp.dot(h, wxa_ref[...], preferred_element_type=F32)
    xc = _conv4(xa_buf, ts, xa, cw_ref[...], cb_ref[...])

    pair = 2 * (width // LRU_HEADS)
    gxs, gas = [], []
    for p in range(LRU_HEADS // 2):
        g = _bdot(xc[:, p * pair:(p + 1) * pair], gw_ref[p])
        gxs.append(g[:, :pair])
        gas.append(g[:, pair:])
    gate_x = _sigmoid(jnp.concatenate(gxs, axis=1) + gxb_ref[...])
    gate_a = _sigmoid(jnp.concatenate(gas, axis=1) + gab_ref[...])
    log_a = (-LRU_C * gate_a) * _softplus(-lam_ref[...])
    a = jnp.exp(log_a)
    m = jnp.maximum(1.0 - a * a, 0.0)
    mult = m * lax.rsqrt(jnp.maximum(m, TINY))
    b = xc * gate_x * mult

    a3 = a.reshape(ts // SUBLANES, SUBLANES, width)
    b3 = b.reshape(ts // SUBLANES, SUBLANES, width)
    sub = lax.broadcasted_iota(jnp.int32, (1, SUBLANES, width), 1)
    for d in (1, 2, 4):
        keep = sub >= d
        a_sh = jnp.where(keep, pltpu.roll(a3, d, 1), 1.0)
        b_sh = jnp.where(keep, pltpu.roll(b3, d, 1), 0.0)
        b3 = a3 * b_sh + b3
        a3 = a3 * a_sh
    a_buf[...] = a3.reshape(ts, width)
    b_buf[...] = b3.reshape(ts, width)

    def group(gi, hc):
        rows = pl.ds(pl.multiple_of(gi * SUBLANES, SUBLANES), SUBLANES)
        hg = b_buf[rows, :] + a_buf[rows, :] * hc
        h_buf[rows, :] = hg
        return jnp.broadcast_to(hg[SUBLANES - 1:SUBLANES, :], (SUBLANES, width))

    carry[...] = lax.fori_loop(0, ts // SUBLANES, group, carry[...])

    ya = jnp.dot(h, wya_ref[...], preferred_element_type=F32)
    out = h_buf[...] * _gelu(ya)
    y = _bdot(out, wp_ref[...])
    gate = _sigmoid(jnp.dot(h, wg_ref[...], preferred_element_type=F32))
    o_ref[...] = (gate * y).astype(o_ref.dtype)


def _lru_branch(x2d, bsz, seq, nrm, wxa, wya, wg, cw, cb, gw, gxb, gab, lam, wp):
    t, d = x2d.shape
    width = cw.shape[1]
    ts = min(512, seq)
    nj = seq // ts
    row = lambda b, j: (b * nj + j, 0)
    consts = (nrm, wxa, wya, wg, cw, cb, gw, gxb, gab, lam, wp)
    return pl.pallas_call(
        _lru_kernel,
        grid=(bsz, nj),
        in_specs=[pl.BlockSpec((ts, d), row)] + [_spec(c) for c in consts],
        out_specs=pl.BlockSpec((ts, d), row),
        out_shape=jax.ShapeDtypeStruct((t, d), BF16),
        scratch_shapes=[pltpu.VMEM((ts + CONV_PAD, width), F32), pltpu.VMEM((ts, width), F32),
                        pltpu.VMEM((ts, width), F32), pltpu.VMEM((ts, width), F32),
                        pltpu.VMEM((SUBLANES, width), F32)],
        compiler_params=_cparams("arbitrary", "arbitrary"),
        name="lru_branch",
    )(x2d, *[_arr(c) for c in consts])


def _sgu_kernel(x_ref, nrm_ref, wu_ref, wv_ref, wg_ref, ng_ref, ws_ref, bias_ref, wp_ref, o_ref, mix_buf):
    ts, width = mix_buf.shape
    gdim = width // SGU_GROUPS
    h = _rms(x_ref[...], nrm_ref[...]).astype(BF16)
    u = _gelu(jnp.dot(h, wu_ref[...], preferred_element_type=F32))
    v = _gelu(jnp.dot(h, wv_ref[...], preferred_element_type=F32))
    vc = v - jnp.mean(v, axis=-1, keepdims=True)
    v = (vc * lax.rsqrt(jnp.mean(vc * vc, axis=-1, keepdims=True) + EPS) * ng_ref[...]).astype(BF16)

    ti = lax.broadcasted_iota(jnp.int32, (SGU_BLOCK, SGU_BLOCK), 0) // SGU_CHUNK
    si = lax.broadcasted_iota(jnp.int32, (SGU_BLOCK, SGU_BLOCK), 1) // SGU_CHUNK
    mask = si <= ti
    wm = [jnp.where(mask, ws_ref[g], 0.0).astype(BF16) for g in range(SGU_GROUPS)]
    for blk in range(ts // SGU_BLOCK):
        rows = slice(blk * SGU_BLOCK, (blk + 1) * SGU_BLOCK)
        for g in range(SGU_GROUPS):
            cols = slice(g * gdim, (g + 1) * gdim)
            mix_buf[rows, cols] = jnp.dot(wm[g], v[rows, cols], preferred_element_type=F32) + bias_ref[:, cols]
    out = u * mix_buf[...]
    y = _bdot(out, wp_ref[...])
    gate = _sigmoid(jnp.dot(h, wg_ref[...], preferred_element_type=F32))
    o_ref[...] = (gate * y).astype(o_ref.dtype)


def _sgu_branch(x2d, seq, nrm, wu, wv, wg, ng, ws, bias, wp):
    t, d = x2d.shape
    width = ng.shape[1]
    ts = min(512, seq)
    consts = (nrm, wu, wv, wg, ng, ws, bias, wp)
    return pl.pallas_call(
        _sgu_kernel,
        grid=(t // ts,),
        in_specs=[pl.BlockSpec((ts, d), lambda i: (i, 0))] + [_spec(c) for c in consts],
        out_specs=pl.BlockSpec((ts, d), lambda i: (i, 0)),
        out_shape=jax.ShapeDtypeStruct((t, d), BF16),
        scratch_shapes=[pltpu.VMEM((ts, width), F32)],
        compiler_params=_cparams("arbitrary"),
        name="sgu_branch",
    )(x2d, *[_arr(c) for c in consts])


def _ssd_kernel(x_ref, nrm_ref, wz_ref, wx_ref, wbc_ref, wdt_ref, wg_ref, cw_ref, cb_ref, dtb_ref, alog_ref,
                dskip_ref, expand_ref, ng_ref, wp_ref, o_ref, xbc_buf, y_buf, state):
    ts, width = y_buf.shape
    lc = SSD_CHUNK
    n_state = SSM_STATE
    gw = width // SSM_GROUPS
    hpg = SSM_HEADS // SSM_GROUPS
    p_dim = SSM_HEAD_DIM
    j = pl.program_id(1)

    @pl.when(j == 0)
    def _():
        xbc_buf[pl.ds(0, CONV_PAD), :] = jnp.zeros((CONV_PAD, xbc_buf.shape[1]), F32)
        state[...] = jnp.zeros_like(state)

    h = _rms(x_ref[...], nrm_ref[...]).astype(BF16)
    xbc = jnp.concatenate([jnp.dot(h, wx_ref[...], preferred_element_type=F32),
                           jnp.dot(h, wbc_ref[...], preferred_element_type=F32)], axis=1)
    xc = _silu(_conv4(xbc_buf, ts, xbc, cw_ref[...], cb_ref[...]))
    lane = lax.broadcasted_iota(jnp.int32, (1, LANES), 1)
    heads = lane < SSM_HEADS
    dt = jnp.where(heads, _softplus(jnp.dot(h, wdt_ref[...], preferred_element_type=F32) + dtb_ref[...]), 0.0)
    a_row = jnp.where(heads, -jnp.exp(alog_ref[...]), 0.0)
    da = dt * a_row

    li = lax.broadcasted_iota(jnp.int32, (lc, lc), 0)
    si = lax.broadcasted_iota(jnp.int32, (lc, lc), 1)
    causal = li >= si
    tri = causal.astype(BF16)
    plane = lax.broadcasted_iota(jnp.int32, (lc, 2 * p_dim), 1)
    expand = expand_ref[...]

    for c in range(ts // lc):
        rows = slice(c * lc, (c + 1) * lc)
        dt_c = dt[rows]
        cs = _dot_exact_lhs(tri, da[rows])
        cs_t = cs.T
        dt_t = dt_c.T
        cs_last = cs[lc - 1:lc, :]
        wst_e = _bdot(jnp.exp(cs_last - cs) * dt_c, expand)
        ecs_e = _bdot(jnp.exp(cs), expand)
        cd_e = _dot_exact_rhs(jnp.broadcast_to(jnp.exp(cs_last), (SUBLANES, LANES)), expand)[0:1, :]
        xs_c = xc[rows, :width]
        xw = (xs_c * wst_e).astype(BF16)
        xs_b = xs_c.astype(BF16)
        for g in range(SSM_GROUPS):
            bm = xc[rows, width + g * n_state: width + (g + 1) * n_state]
            cm = xc[rows, width + (SSM_GROUPS + g) * n_state: width + (SSM_GROUPS + g + 1) * n_state].astype(BF16)
            cb = _bdot_nt(cm, bm)
            st = state[g]
            y_g = jnp.dot(cm, st.astype(BF16), preferred_element_type=F32) * ecs_e[:, g * gw:(g + 1) * gw]
            yds = []
            for pr in range(hpg // 2):
                ws = []
                for r in (2 * pr, 2 * pr + 1):
                    hh = g * hpg + r
                    seg = cs[:, hh:hh + 1] - cs_t[hh:hh + 1, :]
                    ws.append((cb * jnp.exp(jnp.where(causal, seg, -jnp.inf)) * dt_t[hh:hh + 1, :]).astype(BF16))
                xp = xs_b[:, g * gw + pr * 2 * p_dim: g * gw + (pr + 1) * 2 * p_dim]
                zero = jnp.zeros_like(xp)
                rhs = jnp.concatenate([jnp.where(plane < p_dim, xp, zero), jnp.where(plane >= p_dim, xp, zero)], axis=0)
                yds.append(jnp.dot(jnp.concatenate(ws, axis=1), rhs, preferred_element_type=F32))
            y_buf[rows, g * gw:(g + 1) * gw] = y_g + jnp.concatenate(yds, axis=1)
            new = jnp.dot(bm.T.astype(BF16), xw[:, g * gw:(g + 1) * gw], preferred_element_type=F32)
            state[g] = st * cd_e[:, g * gw:(g + 1) * gw] + new

    z = jnp.dot(h, wz_ref[...], preferred_element_type=F32)
    y = y_buf[...] + xc[:, :width] * dskip_ref[...]
    yg = y * _silu(z)
    ng = ng_ref[...]
    parts = []
    for g in range(SSM_GROUPS):
        seg = yg[:, g * gw:(g + 1) * gw]
        parts.append(seg * lax.rsqrt(jnp.mean(seg * seg, axis=-1, keepdims=True) + EPS) * ng[:, g * gw:(g + 1) * gw])
    yn = jnp.concatenate(parts, axis=1)
    out = _bdot(yn, wp_ref[...])
    gate = _sigmoid(jnp.dot(h, wg_ref[...], preferred_element_type=F32))
    o_ref[...] = (gate * out).astype(o_ref.dtype)


def _ssd_branch(x2d, bsz, seq, nrm, wz, wx, wbc, wdt, wg, cw, cb, dtb, alog, dskip, expand, ng, wp):
    t, d = x2d.shape
    width = ng.shape[1]
    ts = min(256, seq)
    nj = seq // ts
    row = lambda b, j: (b * nj + j, 0)
    consts = (nrm, wz, wx, wbc, wdt, wg, cw, cb, dtb, alog, dskip, expand, ng, wp)
    return pl.pallas_call(
        _ssd_kernel,
        grid=(bsz, nj),
        in_specs=[pl.BlockSpec((ts, d), row)] + [_spec(c) for c in consts],
        out_specs=pl.BlockSpec((ts, d), row),
        out_shape=jax.ShapeDtypeStruct((t, d), BF16),
        scratch_shapes=[pltpu.VMEM((ts + CONV_PAD, cw.shape[1]), F32), pltpu.VMEM((ts, width), F32),
                        pltpu.VMEM((SSM_GROUPS, SSM_STATE, width // SSM_GROUPS), F32)],
        compiler_params=_cparams("arbitrary", "arbitrary"),
        name="ssd_branch",
    )(x2d, *[_arr(c) for c in consts])


def _kv_kernel(m_ref, nrm_ref, w_ref, k_ref, v_ref):
    d = k_ref.shape[1]
    h = _rms(m_ref[...], nrm_ref[...]).astype(BF16)
    kv = jnp.dot(h, w_ref[...], preferred_element_type=F32)
    k_ref[...] = kv[:, :d].astype(k_ref.dtype)
    v_ref[...] = kv[:, d:].astype(v_ref.dtype)


def _kv_proj(mem2d, nrm, wkv):
    t, d = mem2d.shape
    ts = min(512, t)
    return pl.pallas_call(
        _kv_kernel,
        grid=(t // ts,),
        in_specs=[pl.BlockSpec((ts, d), lambda i: (i, 0)), _full(nrm), _full(wkv)],
        out_specs=[pl.BlockSpec((ts, d), lambda i: (i, 0))] * 2,
        out_shape=[jax.ShapeDtypeStruct((t, d), BF16)] * 2,
        compiler_params=_cparams("arbitrary"),
        name="kv_proj",
    )(mem2d, nrm, wkv)


def _attn_kernel(x_ref, ma_ref, mb_ref, mc_ref, k_ref, v_ref, wout_ref, nxa_ref, wq_ref, wo_ref, nffn_ref,
                 wr_ref, br_ref, x2_ref, hf_ref, lg_ref):
    d = x_ref.shape[1]
    hd = d // XA_HEADS
    merged = ma_ref[...].astype(F32) + mb_ref[...].astype(F32) + mc_ref[...].astype(F32)
    x1 = x_ref[...] + _bdot(merged, wout_ref[...])
    q = (_bdot(_rms(x1, nxa_ref[...]), wq_ref[...]) * (hd ** -0.5)).astype(BF16)
    outs = []
    for hh in range(XA_HEADS):
        cols = slice(hh * hd, (hh + 1) * hd)
        s = _bdot_nt(q[:, cols], k_ref[:, cols])
        s = jnp.exp(s - jnp.max(s, axis=-1, keepdims=True))
        p = s / jnp.sum(s, axis=-1, keepdims=True)
        outs.append(_bdot(p, v_ref[:, cols]))
    x2 = x1 + _bdot(jnp.concatenate(outs, axis=1), wo_ref[...])
    x2_ref[...] = x2
    hf = _rms(x2, nffn_ref[...])
    hf_ref[...] = hf
    lg_ref[...] = _bdot_nt(wr_ref[...], hf) + br_ref[...]


def _attn_block(x2d, ma, mb, mc, k2d, v2d, bsz, seq, mem_len, wout, nxa, wq, wo, nffn, wr_t, br_col):
    t, d = x2d.shape
    ts = min(512, seq)
    nj = seq // ts
    row = lambda i: (i, 0)
    memrow = lambda i: (i // nj, 0)
    consts = (wout, nxa, wq, wo, nffn, wr_t, br_col)
    return pl.pallas_call(
        _attn_kernel,
        grid=(t // ts,),
        in_specs=[pl.BlockSpec((ts, d), row)] * 4 + [pl.BlockSpec((mem_len, d), memrow)] * 2
                 + [_spec(c) for c in consts],
        out_specs=[pl.BlockSpec((ts, d), row), pl.BlockSpec((ts, d), row), pl.BlockSpec((LANES, ts), lambda i: (0, i))],
        out_shape=[jax.ShapeDtypeStruct((t, d), F32), jax.ShapeDtypeStruct((t, d), F32),
                   jax.ShapeDtypeStruct((LANES, t), F32)],
        compiler_params=_cparams("arbitrary"),
        name="attn_block",
    )(x2d, ma, mb, mc, k2d, v2d, *consts)


def _first_max(rows):
    best = rows[0]
    for r in rows[1:]:
        best = jnp.maximum(best, r)
    idx = jnp.full(best.shape, len(rows) - 1, jnp.int32)
    for i in range(len(rows) - 2, -1, -1):
        idx = jnp.where(rows[i] >= best, i, idx)
    return best, idx


def _route_kernel(lg_ref, ids_ref, wts_ref, cnt_ref, carry):
    ts = lg_ref.shape[1]
    i = pl.program_id(0)

    @pl.when(i == 0)
    def _():
        carry[...] = jnp.zeros_like(carry)

    lg = lg_ref[...]
    grows = [lg[g:g + 1, :] for g in range(N_EXPERT_GROUPS)]
    gmax, grp = _first_max(grows)
    denom = grows[0] * 0.0
    for r in grows:
        denom = denom + jnp.exp(r - gmax)
    p_grp = 1.0 / denom

    erows = []
    for e in range(EXPERTS_PER_GROUP):
        r_last = N_EXPERT_GROUPS + (N_EXPERT_GROUPS - 1) * EXPERTS_PER_GROUP + e
        sel = lg[r_last:r_last + 1, :]
        for g in range(N_EXPERT_GROUPS - 2, -1, -1):
            r0 = N_EXPERT_GROUPS + g * EXPERTS_PER_GROUP + e
            sel = jnp.where(grp == g, lg[r0:r0 + 1, :], sel)
        erows.append(sel)
    v1, i1 = _first_max(erows)
    rest = [jnp.where(i1 == e, -jnp.inf, erows[e]) for e in range(EXPERTS_PER_GROUP)]
    v2, i2 = _first_max(rest)
    tt = jnp.exp(v2 - v1)
    w1 = p_grp / (1.0 + tt)
    w2 = p_grp * tt / (1.0 + tt)
    e1 = grp * EXPERTS_PER_GROUP + i1
    e2 = grp * EXPERTS_PER_GROUP + i2

    sub = lax.broadcasted_iota(jnp.int32, (N_EXPERTS, ts), 0)
    oh1 = sub == e1
    oh2 = sub == e2
    oh = jnp.where(oh1 | oh2, 1.0, 0.0)
    ti = lax.broadcasted_iota(jnp.int32, (ts, ts), 0)
    tj = lax.broadcasted_iota(jnp.int32, (ts, ts), 1)
    before = (ti < tj).astype(BF16)
    base = carry[...]
    rank = jnp.dot(oh.astype(BF16), before, preferred_element_type=F32) \
        + jnp.concatenate([base] * (ts // LANES), axis=1)
    rank1 = jnp.sum(jnp.where(oh1, rank, 0.0), axis=0, keepdims=True)
    rank2 = jnp.sum(jnp.where(oh2, rank, 0.0), axis=0, keepdims=True)
    carry[...] = base + jnp.sum(oh, axis=1, keepdims=True)

    zi = jnp.zeros((SUBLANES - 4, ts), jnp.int32)
    ids_ref[...] = jnp.concatenate([e1, e2, rank1.astype(jnp.int32), rank2.astype(jnp.int32), zi], axis=0)
    wts_ref[...] = jnp.concatenate([w1, w2, jnp.zeros((SUBLANES - 2, ts), F32)], axis=0)
    cnt_ref[...] = carry[...].astype(jnp.int32)


def _route(logits_t):
    t = logits_t.shape[1]
    ts = min(512, t)
    return pl.pallas_call(
        _route_kernel,
        grid=(t // ts,),
        in_specs=[pl.BlockSpec((LANES, ts), lambda i: (0, i))],
        out_specs=[pl.BlockSpec((SUBLANES, ts), lambda i: (0, i)), pl.BlockSpec((SUBLANES, ts), lambda i: (0, i)),
                   pl.BlockSpec((N_EXPERTS, LANES), lambda i: (0, 0))],
        out_shape=[jax.ShapeDtypeStruct((SUBLANES, t), jnp.int32), jax.ShapeDtypeStruct((SUBLANES, t), F32),
                   jax.ShapeDtypeStruct((N_EXPERTS, LANES), jnp.int32)],
        scratch_shapes=[pltpu.VMEM((N_EXPERTS, LANES), F32)],
        compiler_params=_cparams("arbitrary"),
        name="moe_route",
    )(logits_t)


def _dest_kernel(ids_ref, ps_ref, dst_ref):
    ts = ids_ref.shape[1]
    ids = ids_ref[...]
    sub = lax.broadcasted_iota(jnp.int32, (N_EXPERTS, ts), 0)
    ps = jnp.concatenate([ps_ref[...]] * (ts // LANES), axis=1)
    d1 = jnp.sum(jnp.where(sub == ids[0:1, :], ps, 0), axis=0, keepdims=True) + ids[2:3, :]
    d2 = jnp.sum(jnp.where(sub == ids[1:2, :], ps, 0), axis=0, keepdims=True) + ids[3:4, :]
    dst_ref[...] = jnp.concatenate([d1, d2, jnp.zeros((SUBLANES - 2, ts), jnp.int32)], axis=0)


def _dest(ids, pad_starts_rep):
    t = ids.shape[1]
    ts = min(2048, t)
    return pl.pallas_call(
        _dest_kernel,
        grid=(t // ts,),
        in_specs=[pl.BlockSpec((SUBLANES, ts), lambda i: (0, i)), _full(pad_starts_rep)],
        out_specs=pl.BlockSpec((SUBLANES, ts), lambda i: (0, i)),
        out_shape=jax.ShapeDtypeStruct((SUBLANES, t), jnp.int32),
        compiler_params=_cparams("arbitrary"),
        name="moe_dest",
    )(ids, pad_starts_rep)


def _hbm_row(ref, r):
    return ref.at[pl.ds(r, 1), :]


def _tile_row(ref3, g, u):
    return ref3.at[g, pl.ds(u, 1), :]


def _for_rows(n_rows, fn):
    def trip(g, c):
        for u in range(SUBLANES):
            fn(g, u)
        return c
    lax.fori_loop(0, n_rows // SUBLANES, trip, 0)


def _dispatch_kernel(d1_ref, d2_ref, pe_ref, hf_ref, xs_ref, ring, zbuf, in_sem, out_sem, z_sem, *, n_tiles):
    tg = ring.shape[1]
    ts = tg * SUBLANES
    n_blocks = xs_ref.shape[0] // MOE_BLOCK
    i = pl.program_id(0)
    slot = i % DISPATCH_SLOTS

    def load(tile, s):
        return pltpu.make_async_copy(hf_ref.at[pl.ds(tile * tg, tg)], ring.at[s], in_sem.at[s])

    def drain(s):
        for _ in range(2):
            pltpu.make_async_copy(hf_ref.at[pl.ds(0, tg)], ring.at[s], out_sem.at[s]).wait()

    def zero_block(first_row):
        rows = pl.ds(pl.multiple_of(first_row, MOE_BLOCK), MOE_BLOCK)
        return pltpu.make_async_copy(zbuf, xs_ref.at[rows, :], z_sem)

    @pl.when(i == 0)
    def _():
        load(0, 0).start()
        zbuf[...] = jnp.zeros_like(zbuf)
        for start in (True, False):
            for e in range(N_EXPERTS):
                prev_end = pe_ref[e - 1] if e else 0

                @pl.when(pe_ref[e] > prev_end)
                def _():
                    cp = zero_block(pe_ref[e] - MOE_BLOCK)
                    cp.start() if start else cp.wait()
            for blk in range(n_blocks - N_EXPERTS, n_blocks):
                @pl.when(blk * MOE_BLOCK >= pe_ref[N_EXPERTS - 1])
                def _():
                    cp = zero_block(blk * MOE_BLOCK)
                    cp.start() if start else cp.wait()

    nxt = (i + 1) % DISPATCH_SLOTS

    @pl.when(i + 1 < n_tiles)
    def _():
        @pl.when(i + 1 >= DISPATCH_SLOTS)
        def _():
            drain(nxt)
        load(i + 1, nxt).start()

    load(i, slot).wait()
    base = i * ts
    src = ring.at[slot]
    sem = out_sem.at[slot]

    def scatter(g, u):
        tok = base + g * SUBLANES + u
        pltpu.make_async_copy(_tile_row(src, g, u), _hbm_row(xs_ref, d1_ref[tok]), sem).start()
        pltpu.make_async_copy(_tile_row(src, g, u), _hbm_row(xs_ref, d2_ref[tok]), sem).start()

    _for_rows(ts, scatter)

    @pl.when(i == n_tiles - 1)
    def _():
        for back in range(min(DISPATCH_SLOTS, n_tiles)):
            drain((n_tiles - 1 - back) % DISPATCH_SLOTS)


def _dispatch(d1, d2, pad_ends, hf, n_slots):
    t, d = hf.shape
    ts = min(256, t)
    n_tiles = t // ts
    hf = hf.reshape(t // SUBLANES, SUBLANES, d)
    grid_spec = pltpu.PrefetchScalarGridSpec(
        num_scalar_prefetch=3,
        grid=(n_tiles,),
        in_specs=[pl.BlockSpec(memory_space=pl.ANY)],
        out_specs=pl.BlockSpec(memory_space=pl.ANY),
        scratch_shapes=[pltpu.VMEM((DISPATCH_SLOTS, ts // SUBLANES, SUBLANES, d), F32), pltpu.VMEM((MOE_BLOCK, d), F32),
                        pltpu.SemaphoreType.DMA((DISPATCH_SLOTS,)), pltpu.SemaphoreType.DMA((DISPATCH_SLOTS,)),
                        pltpu.SemaphoreType.DMA],
    )
    return pl.pallas_call(
        functools.partial(_dispatch_kernel, n_tiles=n_tiles),
        grid_spec=grid_spec,
        out_shape=jax.ShapeDtypeStruct((n_slots, d), F32),
        compiler_params=_cparams("arbitrary"),
        name="moe_dispatch",
    )(d1, d2, pad_ends, hf)


def _expert_kernel(be_ref, nu_ref, xs_ref, wgu_ref, wdn_ref, y_ref, wgu_b, wdn_b):
    b = pl.program_id(0)
    de = wdn_ref.shape[0]

    @pl.when(b < nu_ref[0])
    def _():
        @pl.when((b == 0) | (be_ref[b] != be_ref[jnp.maximum(b - 1, 0)]))
        def _():
            wgu_b[...] = wgu_ref[...].astype(BF16)
            wdn_b[...] = wdn_ref[...].astype(BF16)

        gu = _bdot(xs_ref[...], wgu_b[...])
        act = _silu(gu[:, :de]) * gu[:, de:]
        y_ref[...] = _bdot(act, wdn_b[...])

    @pl.when(b >= nu_ref[0])
    def _():
        y_ref[...] = jnp.zeros_like(y_ref)


def _experts(block_e, n_used, xs, wgu, wdn, layer):
    n_slots, d = xs.shape
    n_blocks = n_slots // MOE_BLOCK

    def blk(b, be, nu):
        return jnp.minimum(b, nu[0] - 1)

    def slab(w):
        return pl.BlockSpec((None, None) + w.shape[2:], lambda b, be, nu: (layer, be[blk(b, be, nu)], 0, 0))

    grid_spec = pltpu.PrefetchScalarGridSpec(
        num_scalar_prefetch=2,
        grid=(n_blocks,),
        in_specs=[pl.BlockSpec((MOE_BLOCK, d), lambda b, be, nu: (blk(b, be, nu), 0)), slab(wgu), slab(wdn)],
        out_specs=pl.BlockSpec((MOE_BLOCK, d), lambda b, be, nu: (b, 0)),
        scratch_shapes=[pltpu.VMEM(wgu.shape[2:], BF16), pltpu.VMEM(wdn.shape[2:], BF16)],
    )
    return pl.pallas_call(
        _expert_kernel,
        grid_spec=grid_spec,
        out_shape=jax.ShapeDtypeStruct((n_slots, d), F32),
        compiler_params=_cparams("arbitrary"),
        name="moe_experts",
    )(block_e, n_used, xs, wgu, wdn)


def _combine_kernel(d1_ref, d2_ref, x_ref, w1_ref, w2_ref, nf_ref, yb_ref, o_ref, buf1, buf2, sem, *,
                    final_norm, n_tiles):
    ts = x_ref.shape[0]
    i = pl.program_id(0)
    slot = i % 2

    def gather(tile, s):
        base = tile * ts
        b1, b2, sm = buf1.at[s], buf2.at[s], sem.at[s]

        def one(g, u):
            tok = base + g * SUBLANES + u
            pltpu.make_async_copy(_hbm_row(yb_ref, d1_ref[tok]), _tile_row(b1, g, u), sm).start()
            pltpu.make_async_copy(_hbm_row(yb_ref, d2_ref[tok]), _tile_row(b2, g, u), sm).start()

        _for_rows(ts, one)

    @pl.when(i == 0)
    def _():
        gather(0, 0)

    @pl.when(i + 1 < n_tiles)
    def _():
        gather(i + 1, 1 - slot)

    pltpu.make_async_copy(buf2.at[1 - slot], buf1.at[slot], sem.at[slot]).wait()
    pltpu.make_async_copy(buf1.at[1 - slot], buf2.at[slot], sem.at[slot]).wait()
    d = x_ref.shape[1]
    out = x_ref[...] + w1_ref[...] * buf1[slot].reshape(ts, d) + w2_ref[...] * buf2[slot].reshape(ts, d)
    if final_norm:
        out = _rms(out, nf_ref[...])
    o_ref[...] = out


def _combine(d1, d2, x2, w1c, w2c, nf, yb, final_norm):
    t, d = x2.shape
    ts = min(256, t)
    row = lambda i, a, b: (i, 0)
    grid_spec = pltpu.PrefetchScalarGridSpec(
        num_scalar_prefetch=2,
        grid=(t // ts,),
        in_specs=[pl.BlockSpec((ts, d), row), pl.BlockSpec((ts, 1), row), pl.BlockSpec((ts, 1), row),
                  pl.BlockSpec((1, d), lambda i, a, b: (0, 0)), pl.BlockSpec(memory_space=pl.ANY)],
        out_specs=pl.BlockSpec((ts, d), row),
        scratch_shapes=[pltpu.VMEM((2, ts // SUBLANES, SUBLANES, d), F32),
                        pltpu.VMEM((2, ts // SUBLANES, SUBLANES, d), F32), pltpu.SemaphoreType.DMA((2,))],
    )
    return pl.pallas_call(
        functools.partial(_combine_kernel, final_norm=final_norm, n_tiles=t // ts),
        grid_spec=grid_spec,
        out_shape=jax.ShapeDtypeStruct((t, d), F32),
        compiler_params=_cparams("arbitrary"),
        name="moe_combine",
    )(d1, d2, x2, w1c, w2c, nf, yb)


def _moe(x2, hf, logits_t, wgu, wdn, layer, nf, final_norm):
    t, d = x2.shape
    ids, wts, counts = _route(logits_t)
    counts = counts[:, 0]
    padded = (counts + MOE_BLOCK - 1) // MOE_BLOCK * MOE_BLOCK
    pad_ends = jnp.cumsum(padded)
    pad_starts = pad_ends - padded
    n_blocks = (2 * t + MOE_BLOCK - 1) // MOE_BLOCK + N_EXPERTS
    block_start = jnp.arange(n_blocks, dtype=jnp.int32) * MOE_BLOCK
    block_e = jnp.minimum(jnp.sum(pad_ends[None, :] <= block_start[:, None], axis=1), N_EXPERTS - 1).astype(jnp.int32)
    n_used = (pad_ends[-1:] // MOE_BLOCK).astype(jnp.int32)
    dst = _dest(ids, jnp.broadcast_to(pad_starts[:, None].astype(jnp.int32), (N_EXPERTS, LANES)))
    d1, d2 = dst[0], dst[1]
    xs = _dispatch(d1, d2, pad_ends.astype(jnp.int32), hf, n_blocks * MOE_BLOCK)
    yb = _experts(block_e, n_used, xs, wgu, wdn, layer)
    return _combine(d1, d2, x2, wts[0].reshape(t, 1), wts[1].reshape(t, 1), nf, yb, final_norm)


def _pair_blockdiag(w):
    z = jnp.zeros_like(w[0::2])
    top = jnp.concatenate([w[0::2], z], axis=2)
    bot = jnp.concatenate([z, w[1::2]], axis=2)
    return jnp.concatenate([top, bot], axis=1)


def kernel(x, mem, norm_mix, w_in, conv_a_w, conv_a_b, lru_gx_w, lru_gx_b, lru_ga_w, lru_ga_b, lru_lambda, sgu_norm, sgu_w, sgu_b, conv_c_w, conv_c_b, ssm_dt_bias, ssm_a_log, ssm_d, ssm_norm, w_a_proj, w_b_proj, w_c_proj, w_out, norm_xa, norm_mem, w_q, w_kv, w_o, norm_ffn, w_router_group, b_router_group, w_router_expert, b_router_expert, w_gate_up, w_down, norm_final):
    bsz, seq, d = x.shape
    mem_len = mem.shape[1]
    depth = w_in.shape[0]
    lru_w = conv_a_w.shape[2]
    sgu_wd = sgu_norm.shape[1]
    ssm_w = ssm_norm.shape[1]
    conv_dim = conv_c_w.shape[2]
    off_ya = lru_w
    off_uv = 2 * lru_w
    off_z = off_uv + 2 * sgu_wd
    off_xbc = off_z + ssm_w
    off_dt = off_xbc + conv_dim
    off_gate = off_dt + SSM_HEADS
    t = bsz * seq
    row = lambda v: v.reshape(1, -1).astype(F32)

    expand = (jnp.arange(LANES)[:, None] == (jnp.arange(ssm_w)[None, :] // SSM_HEAD_DIM)).astype(BF16)

    xcur = x.reshape(t, d)
    mem2d = mem.reshape(bsz * mem_len, d)
    for l in range(depth):
        win = w_in[l].astype(BF16)
        nrm = row(norm_mix[l])
        gates = win[:, off_gate:]
        gw = jnp.concatenate([_pair_blockdiag(lru_gx_w[l]), _pair_blockdiag(lru_ga_w[l])], axis=2).astype(BF16)
        m_a = _lru_branch(xcur, bsz, seq, nrm, _Cols(win, 0, lru_w), _Cols(win, off_ya, lru_w), _Cols(gates, 0, d),
                          conv_a_w[l], row(conv_a_b[l]), gw, row(lru_gx_b[l]), row(lru_ga_b[l]),
                          row(lru_lambda[l]), w_a_proj[l].astype(BF16))
        sgu_bias = jnp.repeat(sgu_b[l].T, sgu_wd // SGU_GROUPS, axis=1)
        m_b = _sgu_branch(xcur, seq, nrm, _Cols(win, off_uv, sgu_wd), _Cols(win, off_uv + sgu_wd, sgu_wd),
                          _Cols(gates, d, d), row(sgu_norm[l]), sgu_w[l], sgu_bias, w_b_proj[l].astype(BF16))
        pad16 = lambda v: jnp.pad(v.reshape(1, -1), ((0, 0), (0, LANES - SSM_HEADS)))
        m_c = _ssd_branch(xcur, bsz, seq, nrm, _Cols(win, off_z, ssm_w), _Cols(win, off_xbc, ssm_w),
                          _Cols(win, off_xbc + ssm_w, conv_dim - ssm_w), _Cols(win, off_dt, LANES),
                          _Cols(gates, 2 * d, d), conv_c_w[l], row(conv_c_b[l]), pad16(ssm_dt_bias[l]),
                          pad16(ssm_a_log[l]), row(jnp.repeat(ssm_d[l], SSM_HEAD_DIM)), expand,
                          row(ssm_norm[l]), w_c_proj[l].astype(BF16))
        k2d, v2d = _kv_proj(mem2d, row(norm_mem[l]), w_kv[l].astype(BF16))
        n_r = N_EXPERT_GROUPS + N_EXPERTS
        wr_t = jnp.pad(jnp.concatenate([w_router_group[l], w_router_expert[l]], axis=1).T,
                       ((0, LANES - n_r), (0, 0))).astype(BF16)
        br_col = jnp.pad(jnp.concatenate([b_router_group[l], b_router_expert[l]]), (0, LANES - n_r)).reshape(LANES, 1)
        x2, hf, logits_t = _attn_block(xcur, m_a, m_b, m_c, k2d, v2d, bsz, seq, mem_len, w_out[l].astype(BF16),
                                       row(norm_xa[l]), w_q[l].astype(BF16), w_o[l].astype(BF16),
                                       row(norm_ffn[l]), wr_t, br_col)
        xcur = _moe(x2, hf, logits_t, w_gate_up, w_down, l, row(norm_final),
                    final_norm=(l == depth - 1))
    return xcur.reshape(bsz, seq, d)
```

```python
import functools
from typing import NamedTuple

import jax
import jax.numpy as jnp
from jax import lax
from jax.experimental import pallas as pl
from jax.experimental.pallas import tpu as pltpu

F32 = jnp.float32
BF16 = jnp.bfloat16

EPS = 1e-6
TINY = float(jnp.finfo(jnp.float32).tiny)
CONV_WIDTH = 4
CONV_PAD = 8
LRU_HEADS = 8
LRU_C = 8.0
SGU_GROUPS = 4
SGU_BLOCK = 128
SGU_CHUNK = 64
SSM_HEADS = 16
SSM_HEAD_DIM = 64
SSM_GROUPS = 4
SSM_STATE = 128
SSD_CHUNK = 128
XA_HEADS = 4
N_EXPERT_GROUPS = 4
EXPERTS_PER_GROUP = 8
N_EXPERTS = N_EXPERT_GROUPS * EXPERTS_PER_GROUP
MOE_BLOCK = 256
LANES = 128
SUBLANES = 8
VMEM_LIMIT_BYTES = 56 * 1024 * 1024


def _cparams(*sem):
    return pltpu.CompilerParams(dimension_semantics=sem, vmem_limit_bytes=VMEM_LIMIT_BYTES)


def _full(a):
    nd = a.ndim
    return pl.BlockSpec(a.shape, lambda *_: (0,) * nd)


class _Cols(NamedTuple):
    arr: jax.Array
    off: int
    width: int


def _spec(c):
    if isinstance(c, _Cols):
        blk, rem = divmod(c.off, c.width)
        assert rem == 0
        return pl.BlockSpec((c.arr.shape[0], c.width), lambda *_: (0, blk))
    return _full(c)


def _arr(c):
    return c.arr if isinstance(c, _Cols) else c


def _bdot(a, b):
    return jnp.dot(a.astype(BF16), b.astype(BF16), preferred_element_type=F32)


def _bdot_nt(a, b):
    return lax.dot_general(a.astype(BF16), b.astype(BF16), (((1,), (1,)), ((), ())),
                           preferred_element_type=F32)


def _split3(v):
    hi = v.astype(BF16)
    r1 = v - hi.astype(F32)
    mid = r1.astype(BF16)
    lo = (r1 - mid.astype(F32)).astype(BF16)
    return hi, mid, lo


def _dot_exact_rhs(v, m01):
    hi, mid, lo = _split3(v)
    d = functools.partial(jnp.dot, preferred_element_type=F32)
    return d(hi, m01) + d(mid, m01) + d(lo, m01)


def _dot_exact_lhs(m01, v):
    hi, mid, lo = _split3(v)
    d = functools.partial(jnp.dot, preferred_element_type=F32)
    return d(m01, hi) + d(m01, mid) + d(m01, lo)


def _rms(x, g):
    return x * lax.rsqrt(jnp.mean(x * x, axis=-1, keepdims=True) + EPS) * g


def _sigmoid(x):
    return 1.0 / (1.0 + jnp.exp(-x))


def _silu(x):
    return x * _sigmoid(x)


def _softplus(x):
    return jnp.maximum(x, 0.0) + jnp.log1p(jnp.exp(-jnp.abs(x)))


def _gelu(x):
    return jax.nn.gelu(x)


def _conv4(buf, tile_rows, xin, cw, cb):
    buf[pl.ds(CONV_PAD, tile_rows), :] = xin
    acc = cb
    for k in range(CONV_WIDTH):
        acc = acc + cw[k:k + 1, :] * buf[pl.ds(CONV_PAD - (CONV_WIDTH - 1) + k, tile_rows), :]
    buf[pl.ds(0, CONV_PAD), :] = buf[pl.ds(tile_rows, CONV_PAD), :]
    return acc


def _lru_kernel(x_ref, nrm_ref, wxa_ref, wya_ref, wg_ref, cw_ref, cb_ref, gw_ref, gxb_ref, gab_ref,
                lam_ref, wp_ref, o_ref, xa_buf, a_buf, b_buf, h_buf, carry):
    ts, width = a_buf.shape
    j = pl.program_id(1)

    @pl.when(j == 0)
    def _():
        xa_buf[pl.ds(0, CONV_PAD), :] = jnp.zeros((CONV_PAD, width), F32)
        carry[...] = jnp.zeros_like(carry)

    h = _rms(x_ref[...], nrm_ref[...]).astype(BF16)
    xa = jnp.dot(h, wxa_ref[...], preferred_element_type=F32)
    xc = _conv4(xa_buf, ts, xa, cw_ref[...], cb_ref[...])

    pair = 2 * (width // LRU_HEADS)
    gxs, gas = [], []
    for p in range(LRU_HEADS // 2):
        g = _bdot(xc[:, p * pair:(p + 1) * pair], gw_ref[p])
        gxs.append(g[:, :pair])
        gas.append(g[:, pair:])
    gate_x = _sigmoid(jnp.concatenate(gxs, axis=1) + gxb_ref[...])
    gate_a = _sigmoid(jnp.concatenate(gas, axis=1) + gab_ref[...])
    log_a = (-LRU_C * gate_a) * _softplus(-lam_ref[...])
    a = jnp.exp(log_a)
    m = jnp.maximum(1.0 - a * a, 0.0)
    mult = m * lax.rsqrt(jnp.maximum(m, TINY))
    b = xc * gate_x * mult

    a3 = a.reshape(ts // SUBLANES, SUBLANES, width)
    b3 = b.reshape(ts // SUBLANES, SUBLANES, width)
    sub = lax.broadcasted_iota(jnp.int32, (1, SUBLANES, width), 1)
    for d in (1, 2, 4):
        keep = sub >= d
        a_sh = jnp.where(keep, pltpu.roll(a3, d, 1), 1.0)
        b_sh = jnp.where(keep, pltpu.roll(b3, d, 1), 0.0)
        b3 = a3 * b_sh + b3
        a3 = a3 * a_sh
    a_buf[...] = a3.reshape(ts, width)
    b_buf[...] = b3.reshape(ts, width)

    def group(gi, hc):
        rows = pl.ds(pl.multiple_of(gi * SUBLANES, SUBLANES), SUBLANES)
        hg = b_buf[rows, :] + a_buf[rows, :] * hc
        h_buf[rows, :] = hg
        return jnp.broadcast_to(hg[SUBLANES - 1:SUBLANES, :], (SUBLANES, width))

    carry[...] = lax.fori_loop(0, ts // SUBLANES, group, carry[...])

    ya = jnp.dot(h, wya_ref[...], preferred_element_type=F32)
    out = h_buf[...] * _gelu(ya)
    y = _bdot(out, wp_ref[...])
    gate = _sigmoid(jnp.dot(h, wg_ref[...], preferred_element_type=F32))
    o_ref[...] = (gate * y).astype(o_ref.dtype)


def _lru_branch(x2d, bsz, seq, nrm, wxa, wya, wg, cw, cb, gw, gxb, gab, lam, wp):
    t, d = x2d.shape
    width = cw.shape[1]
    ts = min(512, seq)
    nj = seq // ts
    row = lambda b, j: (b * nj + j, 0)
    consts = (nrm, wxa, wya, wg, cw, cb, gw, gxb, gab, lam, wp)
    return pl.pallas_call(
        _lru_kernel,
        grid=(bsz, nj),
        in_specs=[pl.BlockSpec((ts, d), row)] + [_spec(c) for c in consts],
        out_specs=pl.BlockSpec((ts, d), row),
        out_shape=jax.ShapeDtypeStruct((t, d), BF16),
        scratch_shapes=[pltpu.VMEM((ts + CONV_PAD, width), F32), pltpu.VMEM((ts, width), F32),
                        pltpu.VMEM((ts, width), F32), pltpu.VMEM((ts, width), F32),
                        pltpu.VMEM((SUBLANES, width), F32)],
        compiler_params=_cparams("arbitrary", "arbitrary"),
        name="lru_branch",
    )(x2d, *[_arr(c) for c in consts])


def _sgu_kernel(x_ref, *rest):
    _sgu_body(x_ref[...], *rest)


def _sgu_combine_kernel(d1_ref, d2_ref, x_ref, w1_ref, w2_ref, yb_ref, *rest, n_tiles):
    *sgu_refs, x3_ref, mix_buf, buf1, buf2, sem = rest
    ts, d = x_ref.shape
    i = pl.program_id(0)
    slot = i % 2

    def row_copies(tile, s, g, u):
        tok = tile * ts + g * SUBLANES + u
        pltpu.make_async_copy(_hbm_row(yb_ref, d1_ref[tok]), _tile_row(buf1.at[s], g, u), sem.at[s]).start()
        pltpu.make_async_copy(_hbm_row(yb_ref, d2_ref[tok]), _tile_row(buf2.at[s], g, u), sem.at[s]).start()

    def wait_rows(s):
        pltpu.make_async_copy(buf2.at[1 - s], buf1.at[s], sem.at[s]).wait()
        pltpu.make_async_copy(buf1.at[1 - s], buf2.at[s], sem.at[s]).wait()

    @pl.when(i == 0)
    def _():
        _for_rows(ts, functools.partial(row_copies, 0, 0))

    wait_rows(slot)
    x = x_ref[...] + w1_ref[...] * buf1[slot].reshape(ts, d) + w2_ref[...] * buf2[slot].reshape(ts, d)
    x3_ref[...] = x
    nxt = jnp.minimum(i + 1, n_tiles - 1)
    for r in range(ts):
        row_copies(nxt, 1 - slot, r // SUBLANES, r % SUBLANES)
    _sgu_body(x, *sgu_refs, mix_buf)

    @pl.when(i == n_tiles - 1)
    def _():
        wait_rows(1 - slot)


def _sgu_body(x, nrm_ref, wu_ref, wv_ref, wg_ref, ng_ref, ws_ref, bias_ref, wp_ref, o_ref, mix_buf):
    ts, width = mix_buf.shape
    gdim = width // SGU_GROUPS
    h = _rms(x, nrm_ref[...]).astype(BF16)
    u = _gelu(jnp.dot(h, wu_ref[...], preferred_element_type=F32))
    v = _gelu(jnp.dot(h, wv_ref[...], preferred_element_type=F32))
    vc = v - jnp.mean(v, axis=-1, keepdims=True)
    v = (vc * lax.rsqrt(jnp.mean(vc * vc, axis=-1, keepdims=True) + EPS) * ng_ref[...]).astype(BF16)

    ti = lax.broadcasted_iota(jnp.int32, (SGU_BLOCK, SGU_BLOCK), 0) // SGU_CHUNK
    si = lax.broadcasted_iota(jnp.int32, (SGU_BLOCK, SGU_BLOCK), 1) // SGU_CHUNK
    mask = si <= ti
    wm = [jnp.where(mask, ws_ref[g], 0.0).astype(BF16) for g in range(SGU_GROUPS)]
    for blk in range(ts // SGU_BLOCK):
        rows = slice(blk * SGU_BLOCK, (blk + 1) * SGU_BLOCK)
        for g in range(SGU_GROUPS):
            cols = slice(g * gdim, (g + 1) * gdim)
            mix_buf[rows, cols] = jnp.dot(wm[g], v[rows, cols], preferred_element_type=F32) + bias_ref[:, cols]
    out = u * mix_buf[...]
    y = _bdot(out, wp_ref[...])
    gate = _sigmoid(jnp.dot(h, wg_ref[...], preferred_element_type=F32))
    o_ref[...] = (gate * y).astype(o_ref.dtype)


def _sgu_branch(x2d, seq, nrm, wu, wv, wg, ng, ws, bias, wp, pending=None):
    t, d = x2d.shape
    width = ng.shape[1]
    ts = min(512, seq)
    consts = (nrm, wu, wv, wg, ng, ws, bias, wp)
    if pending is not None:
        d1, d2, w1c, w2c, yb = pending
        row = lambda i, a, b: (i, 0)
        grid_spec = pltpu.PrefetchScalarGridSpec(
            num_scalar_prefetch=2,
            grid=(t // ts,),
            in_specs=[pl.BlockSpec((ts, d), row), pl.BlockSpec((ts, 1), row), pl.BlockSpec((ts, 1), row),
                      pl.BlockSpec(memory_space=pl.ANY)] + [_spec(c) for c in consts],
            out_specs=[pl.BlockSpec((ts, d), row), pl.BlockSpec((ts, d), row)],
            scratch_shapes=[pltpu.VMEM((ts, width), F32), pltpu.VMEM((2, ts // SUBLANES, SUBLANES, d), F32),
                            pltpu.VMEM((2, ts // SUBLANES, SUBLANES, d), F32), pltpu.SemaphoreType.DMA((2,))],
        )
        return pl.pallas_call(
            functools.partial(_sgu_combine_kernel, n_tiles=t // ts),
            grid_spec=grid_spec,
            out_shape=[jax.ShapeDtypeStruct((t, d), BF16), jax.ShapeDtypeStruct((t, d), F32)],
            compiler_params=_cparams("arbitrary"),
            name="sgu_branch_combine",
        )(d1, d2, x2d, w1c, w2c, yb, *[_arr(c) for c in consts])
    return pl.pallas_call(
        _sgu_kernel,
        grid=(t // ts,),
        in_specs=[pl.BlockSpec((ts, d), lambda i: (i, 0))] + [_spec(c) for c in consts],
        out_specs=pl.BlockSpec((ts, d), lambda i: (i, 0)),
        out_shape=jax.ShapeDtypeStruct((t, d), BF16),
        scratch_shapes=[pltpu.VMEM((ts, width), F32)],
        compiler_params=_cparams("arbitrary"),
        name="sgu_branch",
    )(x2d, *[_arr(c) for c in consts])


def _ssd_kernel(x_ref, nrm_ref, wz_ref, wx_ref, wbc_ref, wdt_ref, wg_ref, cw_ref, cb_ref, dtb_ref, alog_ref,
                dskip_ref, expand_ref, ng_ref, wp_ref, o_ref, xbc_buf, y_buf, state):
    ts, width = y_buf.shape
    lc = SSD_CHUNK
    n_state = SSM_STATE
    gw = width // SSM_GROUPS
    hpg = SSM_HEADS // SSM_GROUPS
    p_dim = SSM_HEAD_DIM
    j = pl.program_id(1)

    @pl.when(j == 0)
    def _():
        xbc_buf[pl.ds(0, CONV_PAD), :] = jnp.zeros((CONV_PAD, xbc_buf.shape[1]), F32)
        state[...] = jnp.zeros_like(state)

    h = _rms(x_ref[...], nrm_ref[...]).astype(BF16)
    xbc = jnp.concatenate([jnp.dot(h, wx_ref[...], preferred_element_type=F32),
                           jnp.dot(h, wbc_ref[...], preferred_element_type=F32)], axis=1)
    xc = _silu(_conv4(xbc_buf, ts, xbc, cw_ref[...], cb_ref[...]))
    lane = lax.broadcasted_iota(jnp.int32, (1, LANES), 1)
    heads = lane < SSM_HEADS
    dt = jnp.where(heads, _softplus(jnp.dot(h, wdt_ref[...], preferred_element_type=F32) + dtb_ref[...]), 0.0)
    a_row = jnp.where(heads, -jnp.exp(alog_ref[...]), 0.0)
    da = dt * a_row

    li = lax.broadcasted_iota(jnp.int32, (lc, lc), 0)
    si = lax.broadcasted_iota(jnp.int32, (lc, lc), 1)
    causal = li >= si
    tri = causal.astype(BF16)
    plane = lax.broadcasted_iota(jnp.int32, (lc, 2 * p_dim), 1)
    expand = expand_ref[...]

    for c in range(ts // lc):
        rows = slice(c * lc, (c + 1) * lc)
        dt_c = dt[rows]
        cs = _dot_exact_lhs(tri, da[rows])
        cs_t = cs.T
        dt_t = dt_c.T
        cs_last = cs[lc - 1:lc, :]
        wst_e = _bdot(jnp.exp(cs_last - cs) * dt_c, expand)
        ecs_e = _bdot(jnp.exp(cs), expand)
        cd_e = _dot_exact_rhs(jnp.broadcast_to(jnp.exp(cs_last), (SUBLANES, LANES)), expand)[0:1, :]
        xs_c = xc[rows, :width]
        xw = (xs_c * wst_e).astype(BF16)
        xs_b = xs_c.astype(BF16)
        for g in range(SSM_GROUPS):
            bm = xc[rows, width + g * n_state: width + (g + 1) * n_state]
            cm = xc[rows, width + (SSM_GROUPS + g) * n_state: width + (SSM_GROUPS + g + 1) * n_state].astype(BF16)
            cb = _bdot_nt(cm, bm)
            st = state[g]
            y_g = jnp.dot(cm, st.astype(BF16), preferred_element_type=F32) * ecs_e[:, g * gw:(g + 1) * gw]
            yds = []
            for pr in range(hpg // 2):
                ws = []
                for r in (2 * pr, 2 * pr + 1):
                    hh = g * hpg + r
                    seg = cs[:, hh:hh + 1] - cs_t[hh:hh + 1, :]
                    ws.append((cb * jnp.exp(jnp.where(causal, seg, -jnp.inf)) * dt_t[hh:hh + 1, :]).astype(BF16))
                xp = xs_b[:, g * gw + pr * 2 * p_dim: g * gw + (pr + 1) * 2 * p_dim]
                zero = jnp.zeros_like(xp)
                rhs = jnp.concatenate([jnp.where(plane < p_dim, xp, zero), jnp.where(plane >= p_dim, xp, zero)], axis=0)
                yds.append(jnp.dot(jnp.concatenate(ws, axis=1), rhs, preferred_element_type=F32))
            y_buf[rows, g * gw:(g + 1) * gw] = y_g + jnp.concatenate(yds, axis=1)
            new = jnp.dot(bm.T.astype(BF16), xw[:, g * gw:(g + 1) * gw], preferred_element_type=F32)
            state[g] = st * cd_e[:, g * gw:(g + 1) * gw] + new

    z = jnp.dot(h, wz_ref[...], preferred_element_type=F32)
    y = y_buf[...] + xc[:, :width] * dskip_ref[...]
    yg = y * _silu(z)
    ng = ng_ref[...]
    parts = []
    for g in range(SSM_GROUPS):
        seg = yg[:, g * gw:(g + 1) * gw]
        parts.append(seg * lax.rsqrt(jnp.mean(seg * seg, axis=-1, keepdims=True) + EPS) * ng[:, g * gw:(g + 1) * gw])
    yn = jnp.concatenate(parts, axis=1)
    out = _bdot(yn, wp_ref[...])
    gate = _sigmoid(jnp.dot(h, wg_ref[...], preferred_element_type=F32))
    o_ref[...] = (gate * out).astype(o_ref.dtype)


def _ssd_branch(x2d, bsz, seq, nrm, wz, wx, wbc, wdt, wg, cw, cb, dtb, alog, dskip, expand, ng, wp):
    t, d = x2d.shape
    width = ng.shape[1]
    ts = min(256, seq)
    nj = seq // ts
    row = lambda b, j: (b * nj + j, 0)
    consts = (nrm, wz, wx, wbc, wdt, wg, cw, cb, dtb, alog, dskip, expand, ng, wp)
    return pl.pallas_call(
        _ssd_kernel,
        grid=(bsz, nj),
        in_specs=[pl.BlockSpec((ts, d), row)] + [_spec(c) for c in consts],
        out_specs=pl.BlockSpec((ts, d), row),
        out_shape=jax.ShapeDtypeStruct((t, d), BF16),
        scratch_shapes=[pltpu.VMEM((ts + CONV_PAD, cw.shape[1]), F32), pltpu.VMEM((ts, width), F32),
                        pltpu.VMEM((SSM_GROUPS, SSM_STATE, width // SSM_GROUPS), F32)],
        compiler_params=_cparams("arbitrary", "arbitrary"),
        name="ssd_branch",
    )(x2d, *[_arr(c) for c in consts])


def _kv_kernel(m_ref, nrm_ref, w_ref, k_ref, v_ref):
    d = k_ref.shape[1]
    h = _rms(m_ref[...], nrm_ref[...]).astype(BF16)
    kv = jnp.dot(h, w_ref[...], preferred_element_type=F32)
    k_ref[...] = kv[:, :d].astype(k_ref.dtype)
    v_ref[...] = kv[:, d:].astype(v_ref.dtype)


def _kv_proj(mem2d, nrm, wkv):
    t, d = mem2d.shape
    ts = min(512, t)
    return pl.pallas_call(
        _kv_kernel,
        grid=(t // ts,),
        in_specs=[pl.BlockSpec((ts, d), lambda i: (i, 0)), _full(nrm), _full(wkv)],
        out_specs=[pl.BlockSpec((ts, d), lambda i: (i, 0))] * 2,
        out_shape=[jax.ShapeDtypeStruct((t, d), BF16)] * 2,
        compiler_params=_cparams("arbitrary"),
        name="kv_proj",
    )(mem2d, nrm, wkv)


def _attn_kernel(x_ref, ma_ref, mb_ref, mc_ref, k_ref, v_ref, wout_ref, nxa_ref, wq_ref, wo_ref, nffn_ref,
                 wr_ref, br_ref, x2_ref, hf_ref, lg_ref):
    d = x_ref.shape[1]
    hd = d // XA_HEADS
    merged = ma_ref[...].astype(F32) + mb_ref[...].astype(F32) + mc_ref[...].astype(F32)
    x1 = x_ref[...] + _bdot(merged, wout_ref[...])
    q = (_bdot(_rms(x1, nxa_ref[...]), wq_ref[...]) * (hd ** -0.5)).astype(BF16)
    outs = []
    for hh in range(XA_HEADS):
        cols = slice(hh * hd, (hh + 1) * hd)
        s = _bdot_nt(q[:, cols], k_ref[:, cols])
        s = jnp.exp(s - jnp.max(s, axis=-1, keepdims=True))
        p = s / jnp.sum(s, axis=-1, keepdims=True)
        outs.append(_bdot(p, v_ref[:, cols]))
    x2 = x1 + _bdot(jnp.concatenate(outs, axis=1), wo_ref[...])
    x2_ref[...] = x2
    hf = _rms(x2, nffn_ref[...])
    hf_ref[...] = hf
    lg_ref[...] = _bdot_nt(wr_ref[...], hf) + br_ref[...]


def _attn_block(x2d, ma, mb, mc, k2d, v2d, bsz, seq, mem_len, wout, nxa, wq, wo, nffn, wr_t, br_col):
    t, d = x2d.shape
    ts = min(512, seq)
    nj = seq // ts
    row = lambda i: (i, 0)
    memrow = lambda i: (i // nj, 0)
    consts = (wout, nxa, wq, wo, nffn, wr_t, br_col)
    return pl.pallas_call(
        _attn_kernel,
        grid=(t // ts,),
        in_specs=[pl.BlockSpec((ts, d), row)] * 4 + [pl.BlockSpec((mem_len, d), memrow)] * 2
                 + [_spec(c) for c in consts],
        out_specs=[pl.BlockSpec((ts, d), row), pl.BlockSpec((ts, d), row), pl.BlockSpec((LANES, ts), lambda i: (0, i))],
        out_shape=[jax.ShapeDtypeStruct((t, d), F32), jax.ShapeDtypeStruct((t, d), F32),
                   jax.ShapeDtypeStruct((LANES, t), F32)],
        compiler_params=_cparams("arbitrary"),
        name="attn_block",
    )(x2d, ma, mb, mc, k2d, v2d, *consts)


def _first_max(rows):
    best = rows[0]
    for r in rows[1:]:
        best = jnp.maximum(best, r)
    idx = jnp.full(best.shape, len(rows) - 1, jnp.int32)
    for i in range(len(rows) - 2, -1, -1):
        idx = jnp.where(rows[i] >= best, i, idx)
    return best, idx


def _route_kernel(lg_ref, ids_ref, wts_ref, cnt_ref, carry):
    ts = lg_ref.shape[1]
    i = pl.program_id(0)

    @pl.when(i == 0)
    def _():
        carry[...] = jnp.zeros_like(carry)

    lg = lg_ref[...]
    grows = [lg[g:g + 1, :] for g in range(N_EXPERT_GROUPS)]
    gmax, grp = _first_max(grows)
    denom = grows[0] * 0.0
    for r in grows:
        denom = denom + jnp.exp(r - gmax)
    p_grp = 1.0 / denom

    erows = []
    for e in range(EXPERTS_PER_GROUP):
        r_last = N_EXPERT_GROUPS + (N_EXPERT_GROUPS - 1) * EXPERTS_PER_GROUP + e
        sel = lg[r_last:r_last + 1, :]
        for g in range(N_EXPERT_GROUPS - 2, -1, -1):
            r0 = N_EXPERT_GROUPS + g * EXPERTS_PER_GROUP + e
            sel = jnp.where(grp == g, lg[r0:r0 + 1, :], sel)
        erows.append(sel)
    v1, i1 = _first_max(erows)
    rest = [jnp.where(i1 == e, -jnp.inf, erows[e]) for e in range(EXPERTS_PER_GROUP)]
    v2, i2 = _first_max(rest)
    tt = jnp.exp(v2 - v1)
    w1 = p_grp / (1.0 + tt)
    w2 = p_grp * tt / (1.0 + tt)
    e1 = grp * EXPERTS_PER_GROUP + i1
    e2 = grp * EXPERTS_PER_GROUP + i2

    sub = lax.broadcasted_iota(jnp.int32, (N_EXPERTS, ts), 0)
    oh1 = sub == e1
    oh2 = sub == e2
    oh = jnp.where(oh1 | oh2, 1.0, 0.0)
    ti = lax.broadcasted_iota(jnp.int32, (ts, ts), 0)
    tj = lax.broadcasted_iota(jnp.int32, (ts, ts), 1)
    before = (ti < tj).astype(BF16)
    base = carry[...]
    rank = jnp.dot(oh.astype(BF16), before, preferred_element_type=F32) \
        + jnp.concatenate([base] * (ts // LANES), axis=1)
    rank1 = jnp.sum(jnp.where(oh1, rank, 0.0), axis=0, keepdims=True)
    rank2 = jnp.sum(jnp.where(oh2, rank, 0.0), axis=0, keepdims=True)
    carry[...] = base + jnp.sum(oh, axis=1, keepdims=True)

    zi = jnp.zeros((SUBLANES - 4, ts), jnp.int32)
    ids_ref[...] = jnp.concatenate([e1, e2, rank1.astype(jnp.int32), rank2.astype(jnp.int32), zi], axis=0)
    wts_ref[...] = jnp.concatenate([w1, w2, jnp.zeros((SUBLANES - 2, ts), F32)], axis=0)
    cnt_ref[...] = carry[...].astype(jnp.int32)


def _route(logits_t):
    t = logits_t.shape[1]
    ts = min(512, t)
    return pl.pallas_call(
        _route_kernel,
        grid=(t // ts,),
        in_specs=[pl.BlockSpec((LANES, ts), lambda i: (0, i))],
        out_specs=[pl.BlockSpec((SUBLANES, ts), lambda i: (0, i)), pl.BlockSpec((SUBLANES, ts), lambda i: (0, i)),
                   pl.BlockSpec((N_EXPERTS, LANES), lambda i: (0, 0))],
        out_shape=[jax.ShapeDtypeStruct((SUBLANES, t), jnp.int32), jax.ShapeDtypeStruct((SUBLANES, t), F32),
                   jax.ShapeDtypeStruct((N_EXPERTS, LANES), jnp.int32)],
        scratch_shapes=[pltpu.VMEM((N_EXPERTS, LANES), F32)],
        compiler_params=_cparams("arbitrary"),
        name="moe_route",
    )(logits_t)


def _dest_kernel(ids_ref, ps_ref, dst_ref):
    ts = ids_ref.shape[1]
    ids = ids_ref[...]
    sub = lax.broadcasted_iota(jnp.int32, (N_EXPERTS, ts), 0)
    ps = jnp.concatenate([ps_ref[...]] * (ts // LANES), axis=1)
    d1 = jnp.sum(jnp.where(sub == ids[0:1, :], ps, 0), axis=0, keepdims=True) + ids[2:3, :]
    d2 = jnp.sum(jnp.where(sub == ids[1:2, :], ps, 0), axis=0, keepdims=True) + ids[3:4, :]
    dst_ref[...] = jnp.concatenate([d1, d2, jnp.zeros((SUBLANES - 2, ts), jnp.int32)], axis=0)


def _dest(ids, pad_starts_rep):
    t = ids.shape[1]
    ts = min(2048, t)
    return pl.pallas_call(
        _dest_kernel,
        grid=(t // ts,),
        in_specs=[pl.BlockSpec((SUBLANES, ts), lambda i: (0, i)), _full(pad_starts_rep)],
        out_specs=pl.BlockSpec((SUBLANES, ts), lambda i: (0, i)),
        out_shape=jax.ShapeDtypeStruct((SUBLANES, t), jnp.int32),
        compiler_params=_cparams("arbitrary"),
        name="moe_dest",
    )(ids, pad_starts_rep)


def _hbm_row(ref, r):
    return ref.at[pl.ds(r, 1), :]


def _tile_row(ref3, g, u):
    return ref3.at[g, pl.ds(u, 1), :]


def _for_rows(n_rows, fn):
    def trip(g, c):
        for u in range(SUBLANES):
            fn(g, u)
        return c
    lax.fori_loop(0, n_rows // SUBLANES, trip, 0)


def _expert_kernel(be_ref, nu_ref, d1_ref, d2_ref, hf_ref, zeros_ref, wgu_ref, wdn_ref, y_ref, inv, xbuf, wgu_b, wdn_b,
                   sem, inv_sem):
    b = pl.program_id(0)
    n_tok = d1_ref.shape[0]
    d = y_ref.shape[1]
    de = wdn_ref.shape[0]
    nu = nu_ref[0]
    slot = b % 2

    def gather(block, s):
        base = block * MOE_BLOCK
        dst = xbuf.at[s]
        for r in range(MOE_BLOCK):
            pltpu.make_async_copy(_hbm_row(hf_ref, inv[base + r]), _tile_row(dst, r // SUBLANES, r % SUBLANES),
                                  sem.at[s]).start()

    def wait_rows(s):
        pltpu.make_async_copy(xbuf.at[1 - s], xbuf.at[s], sem.at[s]).wait()

    @pl.when(b == 0)
    def _():
        clear = pltpu.make_async_copy(zeros_ref, inv, inv_sem)
        clear.start()
        clear.wait()

        def fill(i, c):
            for u in range(SUBLANES):
                tok = i * SUBLANES + u
                inv[d1_ref[tok]] = tok
                inv[d2_ref[tok]] = tok
            return c

        lax.fori_loop(0, n_tok // SUBLANES, fill, 0)
        gather(0, 0)

    @pl.when(b < nu)
    def _():
        @pl.when((b == 0) | (be_ref[b] != be_ref[jnp.maximum(b - 1, 0)]))
        def _():
            wgu_b[...] = wgu_ref[...].astype(BF16)
            wdn_b[...] = wdn_ref[...].astype(BF16)

        wait_rows(slot)
        x = xbuf[slot].reshape(MOE_BLOCK, d).astype(BF16)
        gather(jnp.minimum(b + 1, nu - 1), 1 - slot)
        gu = jnp.dot(x, wgu_b[...], preferred_element_type=F32)
        act = _silu(gu[:, :de]) * gu[:, de:]
        y_ref[...] = _bdot(act, wdn_b[...])

        @pl.when(b == nu - 1)
        def _():
            wait_rows(1 - slot)

    @pl.when(b >= nu)
    def _():
        y_ref[...] = jnp.zeros_like(y_ref)


def _experts(block_e, n_used, d1, d2, hf, wgu, wdn, layer, n_slots):
    t, d = hf.shape
    n_blocks = n_slots // MOE_BLOCK

    def slab(w):
        return pl.BlockSpec((None, None) + w.shape[2:],
                            lambda b, be, nu, a1, a2: (layer, be[jnp.minimum(b, nu[0] - 1)], 0, 0))

    grid_spec = pltpu.PrefetchScalarGridSpec(
        num_scalar_prefetch=4,
        grid=(n_blocks,),
        in_specs=[pl.BlockSpec(memory_space=pl.ANY), pl.BlockSpec(memory_space=pl.ANY), slab(wgu), slab(wdn)],
        out_specs=pl.BlockSpec((MOE_BLOCK, d), lambda b, be, nu, a1, a2: (b, 0)),
        scratch_shapes=[pltpu.SMEM((n_slots,), jnp.int32),
                        pltpu.VMEM((2, MOE_BLOCK // SUBLANES, SUBLANES, d), F32),
                        pltpu.VMEM(wgu.shape[2:], BF16), pltpu.VMEM(wdn.shape[2:], BF16),
                        pltpu.SemaphoreType.DMA((2,)), pltpu.SemaphoreType.DMA],
    )
    return pl.pallas_call(
        _expert_kernel,
        grid_spec=grid_spec,
        out_shape=jax.ShapeDtypeStruct((n_slots, d), F32),
        compiler_params=_cparams("arbitrary"),
        name="moe_experts",
    )(block_e, n_used, d1, d2, hf, jnp.zeros((n_slots,), jnp.int32), wgu, wdn)


def _combine_kernel(d1_ref, d2_ref, x_ref, w1_ref, w2_ref, nf_ref, yb_ref, o_ref, buf1, buf2, sem, *, n_tiles):
    ts = x_ref.shape[0]
    i = pl.program_id(0)
    slot = i % 2

    def gather(tile, s):
        base = tile * ts
        b1, b2, sm = buf1.at[s], buf2.at[s], sem.at[s]

        def one(g, u):
            tok = base + g * SUBLANES + u
            pltpu.make_async_copy(_hbm_row(yb_ref, d1_ref[tok]), _tile_row(b1, g, u), sm).start()
            pltpu.make_async_copy(_hbm_row(yb_ref, d2_ref[tok]), _tile_row(b2, g, u), sm).start()

        _for_rows(ts, one)

    @pl.when(i == 0)
    def _():
        gather(0, 0)

    @pl.when(i + 1 < n_tiles)
    def _():
        gather(i + 1, 1 - slot)

    pltpu.make_async_copy(buf2.at[1 - slot], buf1.at[slot], sem.at[slot]).wait()
    pltpu.make_async_copy(buf1.at[1 - slot], buf2.at[slot], sem.at[slot]).wait()
    d = x_ref.shape[1]
    out = x_ref[...] + w1_ref[...] * buf1[slot].reshape(ts, d) + w2_ref[...] * buf2[slot].reshape(ts, d)
    o_ref[...] = _rms(out, nf_ref[...])


def _combine_final(d1, d2, x2, w1c, w2c, nf, yb):
    t, d = x2.shape
    ts = min(256, t)
    row = lambda i, a, b: (i, 0)
    grid_spec = pltpu.PrefetchScalarGridSpec(
        num_scalar_prefetch=2,
        grid=(t // ts,),
        in_specs=[pl.BlockSpec((ts, d), row), pl.BlockSpec((ts, 1), row), pl.BlockSpec((ts, 1), row),
                  pl.BlockSpec((1, d), lambda i, a, b: (0, 0)), pl.BlockSpec(memory_space=pl.ANY)],
        out_specs=pl.BlockSpec((ts, d), row),
        scratch_shapes=[pltpu.VMEM((2, ts // SUBLANES, SUBLANES, d), F32),
                        pltpu.VMEM((2, ts // SUBLANES, SUBLANES, d), F32), pltpu.SemaphoreType.DMA((2,))],
    )
    return pl.pallas_call(
        functools.partial(_combine_kernel, n_tiles=t // ts),
        grid_spec=grid_spec,
        out_shape=jax.ShapeDtypeStruct((t, d), F32),
        compiler_params=_cparams("arbitrary"),
        name="moe_combine",
    )(d1, d2, x2, w1c, w2c, nf, yb)


def _moe_experts(hf, logits_t, wgu, wdn, layer):
    t, d = hf.shape
    ids, wts, counts = _route(logits_t)
    counts = counts[:, 0]
    padded = (counts + MOE_BLOCK - 1) // MOE_BLOCK * MOE_BLOCK
    pad_ends = jnp.cumsum(padded)
    pad_starts = pad_ends - padded
    n_blocks = (2 * t + MOE_BLOCK - 1) // MOE_BLOCK + N_EXPERTS
    block_start = jnp.arange(n_blocks, dtype=jnp.int32) * MOE_BLOCK
    block_e = jnp.minimum(jnp.sum(pad_ends[None, :] <= block_start[:, None], axis=1), N_EXPERTS - 1).astype(jnp.int32)
    n_used = (pad_ends[-1:] // MOE_BLOCK).astype(jnp.int32)
    dst = _dest(ids, jnp.broadcast_to(pad_starts[:, None].astype(jnp.int32), (N_EXPERTS, LANES)))
    d1, d2 = dst[0], dst[1]
    yb = _experts(block_e, n_used, d1, d2, hf, wgu, wdn, layer, n_blocks * MOE_BLOCK)
    return d1, d2, wts[0].reshape(t, 1), wts[1].reshape(t, 1), yb


def _pair_blockdiag(w):
    z = jnp.zeros_like(w[0::2])
    top = jnp.concatenate([w[0::2], z], axis=2)
    bot = jnp.concatenate([z, w[1::2]], axis=2)
    return jnp.concatenate([top, bot], axis=1)


def kernel(x, mem, norm_mix, w_in, conv_a_w, conv_a_b, lru_gx_w, lru_gx_b, lru_ga_w, lru_ga_b, lru_lambda, sgu_norm, sgu_w, sgu_b, conv_c_w, conv_c_b, ssm_dt_bias, ssm_a_log, ssm_d, ssm_norm, w_a_proj, w_b_proj, w_c_proj, w_out, norm_xa, norm_mem, w_q, w_kv, w_o, norm_ffn, w_router_group, b_router_group, w_router_expert, b_router_expert, w_gate_up, w_down, norm_final):
    bsz, seq, d = x.shape
    mem_len = mem.shape[1]
    depth = w_in.shape[0]
    lru_w = conv_a_w.shape[2]
    sgu_wd = sgu_norm.shape[1]
    ssm_w = ssm_norm.shape[1]
    conv_dim = conv_c_w.shape[2]
    off_ya = lru_w
    off_uv = 2 * lru_w
    off_z = off_uv + 2 * sgu_wd
    off_xbc = off_z + ssm_w
    off_dt = off_xbc + conv_dim
    off_gate = off_dt + SSM_HEADS
    t = bsz * seq
    row = lambda v: v.reshape(1, -1).astype(F32)

    expand = (jnp.arange(LANES)[:, None] == (jnp.arange(ssm_w)[None, :] // SSM_HEAD_DIM)).astype(BF16)

    xcur = x.reshape(t, d)
    mem2d = mem.reshape(bsz * mem_len, d)
    pending = None
    for l in range(depth):
        win = w_in[l].astype(BF16)
        nrm = row(norm_mix[l])
        gates = win[:, off_gate:]
        gw = jnp.concatenate([_pair_blockdiag(lru_gx_w[l]), _pair_blockdiag(lru_ga_w[l])], axis=2).astype(BF16)
        sgu_bias = jnp.repeat(sgu_b[l].T, sgu_wd // SGU_GROUPS, axis=1)
        m_b = _sgu_branch(xcur, seq, nrm, _Cols(win, off_uv, sgu_wd), _Cols(win, off_uv + sgu_wd, sgu_wd),
                          _Cols(gates, d, d), row(sgu_norm[l]), sgu_w[l], sgu_bias, w_b_proj[l].astype(BF16),
                          pending=pending)
        if pending is not None:
            m_b, xcur = m_b
        m_a = _lru_branch(xcur, bsz, seq, nrm, _Cols(win, 0, lru_w), _Cols(win, off_ya, lru_w), _Cols(gates, 0, d),
                          conv_a_w[l], row(conv_a_b[l]), gw, row(lru_gx_b[l]), row(lru_ga_b[l]),
                          row(lru_lambda[l]), w_a_proj[l].astype(BF16))
        pad16 = lambda v: jnp.pad(v.reshape(1, -1), ((0, 0), (0, LANES - SSM_HEADS)))
        m_c = _ssd_branch(xcur, bsz, seq, nrm, _Cols(win, off_z, ssm_w), _Cols(win, off_xbc, ssm_w),
                          _Cols(win, off_xbc + ssm_w, conv_dim - ssm_w), _Cols(win, off_dt, LANES),
                          _Cols(gates, 2 * d, d), conv_c_w[l], row(conv_c_b[l]), pad16(ssm_dt_bias[l]),
                          pad16(ssm_a_log[l]), row(jnp.repeat(ssm_d[l], SSM_HEAD_DIM)), expand,
                          row(ssm_norm[l]), w_c_proj[l].astype(BF16))
        k2d, v2d = _kv_proj(mem2d, row(norm_mem[l]), w_kv[l].astype(BF16))
        n_r = N_EXPERT_GROUPS + N_EXPERTS
        wr_t = jnp.pad(jnp.concatenate([w_router_group[l], w_router_expert[l]], axis=1).T,
                       ((0, LANES - n_r), (0, 0))).astype(BF16)
        br_col = jnp.pad(jnp.concatenate([b_router_group[l], b_router_expert[l]]), (0, LANES - n_r)).reshape(LANES, 1)
        x2, hf, logits_t = _attn_block(xcur, m_a, m_b, m_c, k2d, v2d, bsz, seq, mem_len, w_out[l].astype(BF16),
                                       row(norm_xa[l]), w_q[l].astype(BF16), w_o[l].astype(BF16),
                                       row(norm_ffn[l]), wr_t, br_col)
        xcur = x2
        pending = _moe_experts(hf, logits_t, w_gate_up, w_down, l)
    d1, d2, w1c, w2c, yb = pending
    return _combine_final(d1, d2, xcur, w1c, w2c, row(norm_final), yb).reshape(bsz, seq, d)
```

```python
import functools
from typing import NamedTuple

import jax
import jax.numpy as jnp
from jax import lax
from jax.experimental import pallas as pl
from jax.experimental.pallas import tpu as pltpu

F32 = jnp.float32
BF16 = jnp.bfloat16

EPS = 1e-6
TINY = float(jnp.finfo(jnp.float32).tiny)
CONV_WIDTH = 4
CONV_PAD = 8
LRU_HEADS = 8
LRU_C = 8.0
SGU_GROUPS = 4
SGU_BLOCK = 128
SGU_CHUNK = 64
SSM_HEADS = 16
SSM_HEAD_DIM = 64
SSM_GROUPS = 4
SSM_STATE = 128
SSD_CHUNK = 128
XA_HEADS = 4
N_EXPERT_GROUPS = 4
EXPERTS_PER_GROUP = 8
N_EXPERTS = N_EXPERT_GROUPS * EXPERTS_PER_GROUP
MOE_BLOCK = 512
EXPERT_ROW_BUFFERS = 3
COMBINE_ROW_BUFFERS = 3
LANES = 128
SUBLANES = 8
VMEM_LIMIT_BYTES = 56 * 1024 * 1024


def _cparams(*sem):
    return pltpu.CompilerParams(dimension_semantics=sem, vmem_limit_bytes=VMEM_LIMIT_BYTES)


def _full(a):
    nd = a.ndim
    return pl.BlockSpec(a.shape, lambda *_: (0,) * nd)


class _Cols(NamedTuple):
    arr: jax.Array
    off: int
    width: int


def _spec(c):
    if isinstance(c, _Cols):
        blk, rem = divmod(c.off, c.width)
        assert rem == 0
        return pl.BlockSpec((c.arr.shape[0], c.width), lambda *_: (0, blk))
    return _full(c)


def _arr(c):
    return c.arr if isinstance(c, _Cols) else c


def _bdot(a, b):
    return jnp.dot(a.astype(BF16), b.astype(BF16), preferred_element_type=F32)


def _bdot_nt(a, b):
    return lax.dot_general(a.astype(BF16), b.astype(BF16), (((1,), (1,)), ((), ())),
                           preferred_element_type=F32)


def _split3(v):
    hi = v.astype(BF16)
    r1 = v - hi.astype(F32)
    mid = r1.astype(BF16)
    lo = (r1 - mid.astype(F32)).astype(BF16)
    return hi, mid, lo


def _dot_exact_rhs(v, m01):
    hi, mid, lo = _split3(v)
    d = functools.partial(jnp.dot, preferred_element_type=F32)
    return d(hi, m01) + d(mid, m01) + d(lo, m01)


def _dot_exact_lhs(m01, v):
    hi, mid, lo = _split3(v)
    d = functools.partial(jnp.dot, preferred_element_type=F32)
    return d(m01, hi) + d(m01, mid) + d(m01, lo)


def _rms(x, g):
    return x * lax.rsqrt(jnp.mean(x * x, axis=-1, keepdims=True) + EPS) * g


def _sigmoid(x):
    return 1.0 / (1.0 + jnp.exp(-x))


def _silu(x):
    return x * _sigmoid(x)


def _softplus(x):
    return jnp.maximum(x, 0.0) + jnp.log1p(jnp.exp(-jnp.abs(x)))


def _gelu(x):
    return jax.nn.gelu(x)


def _conv4(buf, tile_rows, xin, cw, cb):
    buf[pl.ds(CONV_PAD, tile_rows), :] = xin
    acc = cb
    for k in range(CONV_WIDTH):
        acc = acc + cw[k:k + 1, :] * buf[pl.ds(CONV_PAD - (CONV_WIDTH - 1) + k, tile_rows), :]
    buf[pl.ds(0, CONV_PAD), :] = buf[pl.ds(tile_rows, CONV_PAD), :]
    return acc


def _lru_kernel(x_ref, nrm_ref, wxa_ref, wya_ref, wg_ref, cw_ref, cb_ref, gw_ref, gxb_ref, gab_ref,
                lam_ref, wp_ref, o_ref, xa_buf, a_buf, b_buf, h_buf, carry):
    ts, width = a_buf.shape
    j = pl.program_id(1)

    @pl.when(j == 0)
    def _():
        xa_buf[pl.ds(0, CONV_PAD), :] = jnp.zeros((CONV_PAD, width), F32)
        carry[...] = jnp.zeros_like(carry)

    h = _rms(x_ref[...], nrm_ref[...]).astype(BF16)
    xa = jnp.dot(h, wxa_ref[...], preferred_element_type=F32)
    xc = _conv4(xa_buf, ts, xa, cw_ref[...], cb_ref[...])

    pair = 2 * (width // LRU_HEADS)
    gxs, gas = [], []
    for p in range(LRU_HEADS // 2):
        g = _bdot(xc[:, p * pair:(p + 1) * pair], gw_ref[p])
        gxs.append(g[:, :pair])
        gas.append(g[:, pair:])
    gate_x = _sigmoid(jnp.concatenate(gxs, axis=1) + gxb_ref[...])
    gate_a = _sigmoid(jnp.concatenate(gas, axis=1) + gab_ref[...])
    log_a = (-LRU_C * gate_a) * _softplus(-lam_ref[...])
    a = jnp.exp(log_a)
    m = jnp.maximum(1.0 - a * a, 0.0)
    mult = m * lax.rsqrt(jnp.maximum(m, TINY))
    b = xc * gate_x * mult

    a3 = a.reshape(ts // SUBLANES, SUBLANES, width)
    b3 = b.reshape(ts // SUBLANES, SUBLANES, width)
    sub = lax.broadcasted_iota(jnp.int32, (1, SUBLANES, width), 1)
    for d in (1, 2, 4):
        keep = sub >= d
        a_sh = jnp.where(keep, pltpu.roll(a3, d, 1), 1.0)
        b_sh = jnp.where(keep, pltpu.roll(b3, d, 1), 0.0)
        b3 = a3 * b_sh + b3
        a3 = a3 * a_sh
    a_buf[...] = a3.reshape(ts, width)
    b_buf[...] = b3.reshape(ts, width)

    def group(gi, hc):
        rows = pl.ds(pl.multiple_of(gi * SUBLANES, SUBLANES), SUBLANES)
        hg = b_buf[rows, :] + a_buf[rows, :] * hc
        h_buf[rows, :] = hg
        return jnp.broadcast_to(hg[SUBLANES - 1:SUBLANES, :], (SUBLANES, width))

    carry[...] = lax.fori_loop(0, ts // SUBLANES, group, carry[...])

    ya = jnp.dot(h, wya_ref[...], preferred_element_type=F32)
    out = h_buf[...] * _gelu(ya)
    y = _bdot(out, wp_ref[...])
    gate = _sigmoid(jnp.dot(h, wg_ref[...], preferred_element_type=F32))
    o_ref[...] = (gate * y).astype(o_ref.dtype)


def _lru_branch(x2d, bsz, seq, nrm, wxa, wya, wg, cw, cb, gw, gxb, gab, lam, wp):
    t, d = x2d.shape
    width = cw.shape[1]
    ts = min(512, seq)
    nj = seq // ts
    row = lambda b, j: (b * nj + j, 0)
    consts = (nrm, wxa, wya, wg, cw, cb, gw, gxb, gab, lam, wp)
    return pl.pallas_call(
        _lru_kernel,
        grid=(bsz, nj),
        in_specs=[pl.BlockSpec((ts, d), row)] + [_spec(c) for c in consts],
        out_specs=pl.BlockSpec((ts, d), row),
        out_shape=jax.ShapeDtypeStruct((t, d), BF16),
        scratch_shapes=[pltpu.VMEM((ts + CONV_PAD, width), F32), pltpu.VMEM((ts, width), F32),
                        pltpu.VMEM((ts, width), F32), pltpu.VMEM((ts, width), F32),
                        pltpu.VMEM((SUBLANES, width), F32)],
        compiler_params=_cparams("arbitrary", "arbitrary"),
        name="lru_branch",
    )(x2d, *[_arr(c) for c in consts])


def _sgu_kernel(x_ref, *rest):
    _sgu_body(x_ref[...], *rest)


def _sgu_combine_kernel(d1_ref, d2_ref, x_ref, w1_ref, w2_ref, yb_ref, *rest, n_tiles):
    *sgu_refs, x3_ref, mix_buf, buf1, buf2, sem = rest
    ts, d = x_ref.shape
    i = pl.program_id(0)
    n_buf = buf1.shape[0]
    ahead = n_buf - 1
    slot = i % n_buf

    def row_copies(tile, s, g, u):
        tok = tile * ts + g * SUBLANES + u
        pltpu.make_async_copy(_hbm_row(yb_ref, d1_ref[tok]), _tile_row(buf1.at[s], g, u), sem.at[s]).start()
        pltpu.make_async_copy(_hbm_row(yb_ref, d2_ref[tok]), _tile_row(buf2.at[s], g, u), sem.at[s]).start()

    def wait_rows(s):
        other = (s + 1) % n_buf
        pltpu.make_async_copy(buf2.at[other], buf1.at[s], sem.at[s]).wait()
        pltpu.make_async_copy(buf1.at[other], buf2.at[s], sem.at[s]).wait()

    def last(tile):
        return jnp.minimum(tile, n_tiles - 1)

    @pl.when(i == 0)
    def _():
        for k in range(ahead):
            _for_rows(ts, functools.partial(row_copies, last(k), k))

    wait_rows(slot)
    x = x_ref[...] + w1_ref[...] * buf1[slot].reshape(ts, d) + w2_ref[...] * buf2[slot].reshape(ts, d)
    x3_ref[...] = x
    nxt, nxt_slot = last(i + ahead), (i + ahead) % n_buf
    for r in range(ts):
        row_copies(nxt, nxt_slot, r // SUBLANES, r % SUBLANES)
    _sgu_body(x, *sgu_refs, mix_buf)

    @pl.when(i == n_tiles - 1)
    def _():
        for k in range(1, n_buf):
            wait_rows((i + k) % n_buf)


def _sgu_body(x, nrm_ref, wu_ref, wv_ref, wg_ref, ng_ref, ws_ref, bias_ref, wp_ref, o_ref, mix_buf):
    ts, width = mix_buf.shape
    gdim = width // SGU_GROUPS
    h = _rms(x, nrm_ref[...]).astype(BF16)
    u = _gelu(jnp.dot(h, wu_ref[...], preferred_element_type=F32))
    v = _gelu(jnp.dot(h, wv_ref[...], preferred_element_type=F32))
    vc = v - jnp.mean(v, axis=-1, keepdims=True)
    v = (vc * lax.rsqrt(jnp.mean(vc * vc, axis=-1, keepdims=True) + EPS) * ng_ref[...]).astype(BF16)

    ti = lax.broadcasted_iota(jnp.int32, (SGU_BLOCK, SGU_BLOCK), 0) // SGU_CHUNK
    si = lax.broadcasted_iota(jnp.int32, (SGU_BLOCK, SGU_BLOCK), 1) // SGU_CHUNK
    mask = si <= ti
    wm = [jnp.where(mask, ws_ref[g], 0.0).astype(BF16) for g in range(SGU_GROUPS)]
    for blk in range(ts // SGU_BLOCK):
        rows = slice(blk * SGU_BLOCK, (blk + 1) * SGU_BLOCK)
        for g in range(SGU_GROUPS):
            cols = slice(g * gdim, (g + 1) * gdim)
            mix_buf[rows, cols] = jnp.dot(wm[g], v[rows, cols], preferred_element_type=F32) + bias_ref[:, cols]
    out = u * mix_buf[...]
    y = _bdot(out, wp_ref[...])
    gate = _sigmoid(jnp.dot(h, wg_ref[...], preferred_element_type=F32))
    o_ref[...] = (gate * y).astype(o_ref.dtype)


def _sgu_branch(x2d, seq, nrm, wu, wv, wg, ng, ws, bias, wp, pending=None):
    t, d = x2d.shape
    width = ng.shape[1]
    ts = min(512, seq)
    consts = (nrm, wu, wv, wg, ng, ws, bias, wp)
    if pending is not None:
        d1, d2, w1c, w2c, yb = pending
        row = lambda i, a, b: (i, 0)
        grid_spec = pltpu.PrefetchScalarGridSpec(
            num_scalar_prefetch=2,
            grid=(t // ts,),
            in_specs=[pl.BlockSpec((ts, d), row), pl.BlockSpec((ts, 1), row), pl.BlockSpec((ts, 1), row),
                      pl.BlockSpec(memory_space=pl.ANY)] + [_spec(c) for c in consts],
            out_specs=[pl.BlockSpec((ts, d), row), pl.BlockSpec((ts, d), row)],
            scratch_shapes=[pltpu.VMEM((ts, width), F32),
                            pltpu.VMEM((COMBINE_ROW_BUFFERS, ts // SUBLANES, SUBLANES, d), F32),
                            pltpu.VMEM((COMBINE_ROW_BUFFERS, ts // SUBLANES, SUBLANES, d), F32),
                            pltpu.SemaphoreType.DMA((COMBINE_ROW_BUFFERS,))],
        )
        return pl.pallas_call(
            functools.partial(_sgu_combine_kernel, n_tiles=t // ts),
            grid_spec=grid_spec,
            out_shape=[jax.ShapeDtypeStruct((t, d), BF16), jax.ShapeDtypeStruct((t, d), F32)],
            compiler_params=_cparams("arbitrary"),
            name="sgu_branch_combine",
        )(d1, d2, x2d, w1c, w2c, yb, *[_arr(c) for c in consts])
    return pl.pallas_call(
        _sgu_kernel,
        grid=(t // ts,),
        in_specs=[pl.BlockSpec((ts, d), lambda i: (i, 0))] + [_spec(c) for c in consts],
        out_specs=pl.BlockSpec((ts, d), lambda i: (i, 0)),
        out_shape=jax.ShapeDtypeStruct((t, d), BF16),
        scratch_shapes=[pltpu.VMEM((ts, width), F32)],
        compiler_params=_cparams("arbitrary"),
        name="sgu_branch",
    )(x2d, *[_arr(c) for c in consts])


def _ssd_kernel(x_ref, nrm_ref, wz_ref, wx_ref, wbc_ref, wdt_ref, wg_ref, cw_ref, cb_ref, dtb_ref, alog_ref,
                dskip_ref, expand_ref, ng_ref, wp_ref, o_ref, xbc_buf, y_buf, state):
    ts, width = y_buf.shape
    lc = SSD_CHUNK
    n_state = SSM_STATE
    gw = width // SSM_GROUPS
    hpg = SSM_HEADS // SSM_GROUPS
    p_dim = SSM_HEAD_DIM
    j = pl.program_id(1)

    @pl.when(j == 0)
    def _():
        xbc_buf[pl.ds(0, CONV_PAD), :] = jnp.zeros((CONV_PAD, xbc_buf.shape[1]), F32)
        state[...] = jnp.zeros_like(state)

    h = _rms(x_ref[...], nrm_ref[...]).astype(BF16)
    xbc = jnp.concatenate([jnp.dot(h, wx_ref[...], preferred_element_type=F32),
                           jnp.dot(h, wbc_ref[...], preferred_element_type=F32)], axis=1)
    xc = _silu(_conv4(xbc_buf, ts, xbc, cw_ref[...], cb_ref[...]))
    lane = lax.broadcasted_iota(jnp.int32, (1, LANES), 1)
    heads = lane < SSM_HEADS
    dt = jnp.where(heads, _softplus(jnp.dot(h, wdt_ref[...], preferred_element_type=F32) + dtb_ref[...]), 0.0)
    a_row = jnp.where(heads, -jnp.exp(alog_ref[...]), 0.0)
    da = dt * a_row

    li = lax.broadcasted_iota(jnp.int32, (lc, lc), 0)
    si = lax.broadcasted_iota(jnp.int32, (lc, lc), 1)
    causal = li >= si
    tri = causal.astype(BF16)
    plane = lax.broadcasted_iota(jnp.int32, (lc, 2 * p_dim), 1)
    expand = expand_ref[...]

    for c in range(ts // lc):
        rows = slice(c * lc, (c + 1) * lc)
        dt_c = dt[rows]
        cs = _dot_exact_lhs(tri, da[rows])
        cs_t = cs.T
        dt_t = dt_c.T
        cs_last = cs[lc - 1:lc, :]
        wst_e = _bdot(jnp.exp(cs_last - cs) * dt_c, expand)
        ecs_e = _bdot(jnp.exp(cs), expand)
        cd_e = _dot_exact_rhs(jnp.broadcast_to(jnp.exp(cs_last), (SUBLANES, LANES)), expand)[0:1, :]
        xs_c = xc[rows, :width]
        xw = (xs_c * wst_e).astype(BF16)
        xs_b = xs_c.astype(BF16)
        for g in range(SSM_GROUPS):
            bm = xc[rows, width + g * n_state: width + (g + 1) * n_state]
            cm = xc[rows, width + (SSM_GROUPS + g) * n_state: width + (SSM_GROUPS + g + 1) * n_state].astype(BF16)
            cb = _bdot_nt(cm, bm)
            st = state[g]
            y_g = jnp.dot(cm, st.astype(BF16), preferred_element_type=F32) * ecs_e[:, g * gw:(g + 1) * gw]
            yds = []
            for pr in range(hpg // 2):
                ws = []
                for r in (2 * pr, 2 * pr + 1):
                    hh = g * hpg + r
                    seg = cs[:, hh:hh + 1] - cs_t[hh:hh + 1, :]
                    ws.append((cb * jnp.exp(jnp.where(causal, seg, -jnp.inf)) * dt_t[hh:hh + 1, :]).astype(BF16))
                xp = xs_b[:, g * gw + pr * 2 * p_dim: g * gw + (pr + 1) * 2 * p_dim]
                zero = jnp.zeros_like(xp)
                rhs = jnp.concatenate([jnp.where(plane < p_dim, xp, zero), jnp.where(plane >= p_dim, xp, zero)], axis=0)
                yds.append(jnp.dot(jnp.concatenate(ws, axis=1), rhs, preferred_element_type=F32))
            y_buf[rows, g * gw:(g + 1) * gw] = y_g + jnp.concatenate(yds, axis=1)
            new = jnp.dot(bm.T.astype(BF16), xw[:, g * gw:(g + 1) * gw], preferred_element_type=F32)
            state[g] = st * cd_e[:, g * gw:(g + 1) * gw] + new

    z = jnp.dot(h, wz_ref[...], preferred_element_type=F32)
    y = y_buf[...] + xc[:, :width] * dskip_ref[...]
    yg = y * _silu(z)
    ng = ng_ref[...]
    parts = []
    for g in range(SSM_GROUPS):
        seg = yg[:, g * gw:(g + 1) * gw]
        parts.append(seg * lax.rsqrt(jnp.mean(seg * seg, axis=-1, keepdims=True) + EPS) * ng[:, g * gw:(g + 1) * gw])
    yn = jnp.concatenate(parts, axis=1)
    out = _bdot(yn, wp_ref[...])
    gate = _sigmoid(jnp.dot(h, wg_ref[...], preferred_element_type=F32))
    o_ref[...] = (gate * out).astype(o_ref.dtype)


def _ssd_branch(x2d, bsz, seq, nrm, wz, wx, wbc, wdt, wg, cw, cb, dtb, alog, dskip, expand, ng, wp):
    t, d = x2d.shape
    width = ng.shape[1]
    ts = min(256, seq)
    nj = seq // ts
    row = lambda b, j: (b * nj + j, 0)
    consts = (nrm, wz, wx, wbc, wdt, wg, cw, cb, dtb, alog, dskip, expand, ng, wp)
    return pl.pallas_call(
        _ssd_kernel,
        grid=(bsz, nj),
        in_specs=[pl.BlockSpec((ts, d), row)] + [_spec(c) for c in consts],
        out_specs=pl.BlockSpec((ts, d), row),
        out_shape=jax.ShapeDtypeStruct((t, d), BF16),
        scratch_shapes=[pltpu.VMEM((ts + CONV_PAD, cw.shape[1]), F32), pltpu.VMEM((ts, width), F32),
                        pltpu.VMEM((SSM_GROUPS, SSM_STATE, width // SSM_GROUPS), F32)],
        compiler_params=_cparams("arbitrary", "arbitrary"),
        name="ssd_branch",
    )(x2d, *[_arr(c) for c in consts])


def _kv_kernel(m_ref, nrm_ref, w_ref, k_ref, v_ref):
    d = k_ref.shape[1]
    h = _rms(m_ref[...], nrm_ref[...]).astype(BF16)
    kv = jnp.dot(h, w_ref[...], preferred_element_type=F32)
    k_ref[...] = kv[:, :d].astype(k_ref.dtype)
    v_ref[...] = kv[:, d:].astype(v_ref.dtype)


def _kv_proj(mem2d, nrm, wkv):
    t, d = mem2d.shape
    ts = min(512, t)
    return pl.pallas_call(
        _kv_kernel,
        grid=(t // ts,),
        in_specs=[pl.BlockSpec((ts, d), lambda i: (i, 0)), _full(nrm), _full(wkv)],
        out_specs=[pl.BlockSpec((ts, d), lambda i: (i, 0))] * 2,
        out_shape=[jax.ShapeDtypeStruct((t, d), BF16)] * 2,
        compiler_params=_cparams("arbitrary"),
        name="kv_proj",
    )(mem2d, nrm, wkv)


def _attn_kernel(x_ref, ma_ref, mb_ref, mc_ref, k_ref, v_ref, wout_ref, nxa_ref, wq_ref, wo_ref, nffn_ref,
                 wr_ref, br_ref, x2_ref, hf_ref, lg_ref):
    d = x_ref.shape[1]
    hd = d // XA_HEADS
    merged = ma_ref[...].astype(F32) + mb_ref[...].astype(F32) + mc_ref[...].astype(F32)
    x1 = x_ref[...] + _bdot(merged, wout_ref[...])
    q = (_bdot(_rms(x1, nxa_ref[...]), wq_ref[...]) * (hd ** -0.5)).astype(BF16)
    outs = []
    for hh in range(XA_HEADS):
        cols = slice(hh * hd, (hh + 1) * hd)
        s = _bdot_nt(q[:, cols], k_ref[:, cols])
        s = jnp.exp(s - jnp.max(s, axis=-1, keepdims=True))
        p = s / jnp.sum(s, axis=-1, keepdims=True)
        outs.append(_bdot(p, v_ref[:, cols]))
    x2 = x1 + _bdot(jnp.concatenate(outs, axis=1), wo_ref[...])
    x2_ref[...] = x2
    hf = _rms(x2, nffn_ref[...])
    hf_ref[...] = hf
    lg_ref[...] = _bdot_nt(wr_ref[...], hf) + br_ref[...]


def _attn_block(x2d, ma, mb, mc, k2d, v2d, bsz, seq, mem_len, wout, nxa, wq, wo, nffn, wr_t, br_col):
    t, d = x2d.shape
    ts = min(512, seq)
    nj = seq // ts
    row = lambda i: (i, 0)
    memrow = lambda i: (i // nj, 0)
    consts = (wout, nxa, wq, wo, nffn, wr_t, br_col)
    return pl.pallas_call(
        _attn_kernel,
        grid=(t // ts,),
        in_specs=[pl.BlockSpec((ts, d), row)] * 4 + [pl.BlockSpec((mem_len, d), memrow)] * 2
                 + [_spec(c) for c in consts],
        out_specs=[pl.BlockSpec((ts, d), row), pl.BlockSpec((ts, d), row), pl.BlockSpec((LANES, ts), lambda i: (0, i))],
        out_shape=[jax.ShapeDtypeStruct((t, d), F32), jax.ShapeDtypeStruct((t, d), F32),
                   jax.ShapeDtypeStruct((LANES, t), F32)],
        compiler_params=_cparams("arbitrary"),
        name="attn_block",
    )(x2d, ma, mb, mc, k2d, v2d, *consts)


def _first_max(rows):
    best = rows[0]
    for r in rows[1:]:
        best = jnp.maximum(best, r)
    idx = jnp.full(best.shape, len(rows) - 1, jnp.int32)
    for i in range(len(rows) - 2, -1, -1):
        idx = jnp.where(rows[i] >= best, i, idx)
    return best, idx


def _route_kernel(lg_ref, ids_ref, wts_ref, cnt_ref, carry):
    ts = lg_ref.shape[1]
    i = pl.program_id(0)

    @pl.when(i == 0)
    def _():
        carry[...] = jnp.zeros_like(carry)

    lg = lg_ref[...]
    grows = [lg[g:g + 1, :] for g in range(N_EXPERT_GROUPS)]
    gmax, grp = _first_max(grows)
    denom = grows[0] * 0.0
    for r in grows:
        denom = denom + jnp.exp(r - gmax)
    p_grp = 1.0 / denom

    erows = []
    for e in range(EXPERTS_PER_GROUP):
        r_last = N_EXPERT_GROUPS + (N_EXPERT_GROUPS - 1) * EXPERTS_PER_GROUP + e
        sel = lg[r_last:r_last + 1, :]
        for g in range(N_EXPERT_GROUPS - 2, -1, -1):
            r0 = N_EXPERT_GROUPS + g * EXPERTS_PER_GROUP + e
            sel = jnp.where(grp == g, lg[r0:r0 + 1, :], sel)
        erows.append(sel)
    v1, i1 = _first_max(erows)
    rest = [jnp.where(i1 == e, -jnp.inf, erows[e]) for e in range(EXPERTS_PER_GROUP)]
    v2, i2 = _first_max(rest)
    tt = jnp.exp(v2 - v1)
    w1 = p_grp / (1.0 + tt)
    w2 = p_grp * tt / (1.0 + tt)
    e1 = grp * EXPERTS_PER_GROUP + i1
    e2 = grp * EXPERTS_PER_GROUP + i2

    sub = lax.broadcasted_iota(jnp.int32, (N_EXPERTS, ts), 0)
    oh1 = sub == e1
    oh2 = sub == e2
    oh = jnp.where(oh1 | oh2, 1.0, 0.0)
    ti = lax.broadcasted_iota(jnp.int32, (ts, ts), 0)
    tj = lax.broadcasted_iota(jnp.int32, (ts, ts), 1)
    before = (ti < tj).astype(BF16)
    base = carry[...]
    rank = jnp.dot(oh.astype(BF16), before, preferred_element_type=F32) \
        + jnp.concatenate([base] * (ts // LANES), axis=1)
    rank1 = jnp.sum(jnp.where(oh1, rank, 0.0), axis=0, keepdims=True)
    rank2 = jnp.sum(jnp.where(oh2, rank, 0.0), axis=0, keepdims=True)
    carry[...] = base + jnp.sum(oh, axis=1, keepdims=True)

    zi = jnp.zeros((SUBLANES - 4, ts), jnp.int32)
    ids_ref[...] = jnp.concatenate([e1, e2, rank1.astype(jnp.int32), rank2.astype(jnp.int32), zi], axis=0)
    wts_ref[...] = jnp.concatenate([w1, w2, jnp.zeros((SUBLANES - 2, ts), F32)], axis=0)
    cnt_ref[...] = carry[...].astype(jnp.int32)


def _route(logits_t):
    t = logits_t.shape[1]
    ts = min(512, t)
    return pl.pallas_call(
        _route_kernel,
        grid=(t // ts,),
        in_specs=[pl.BlockSpec((LANES, ts), lambda i: (0, i))],
        out_specs=[pl.BlockSpec((SUBLANES, ts), lambda i: (0, i)), pl.BlockSpec((SUBLANES, ts), lambda i: (0, i)),
                   pl.BlockSpec((N_EXPERTS, LANES), lambda i: (0, 0))],
        out_shape=[jax.ShapeDtypeStruct((SUBLANES, t), jnp.int32), jax.ShapeDtypeStruct((SUBLANES, t), F32),
                   jax.ShapeDtypeStruct((N_EXPERTS, LANES), jnp.int32)],
        scratch_shapes=[pltpu.VMEM((N_EXPERTS, LANES), F32)],
        compiler_params=_cparams("arbitrary"),
        name="moe_route",
    )(logits_t)


def _dest_kernel(ids_ref, ps_ref, dst_ref):
    ts = ids_ref.shape[1]
    ids = ids_ref[...]
    sub = lax.broadcasted_iota(jnp.int32, (N_EXPERTS, ts), 0)
    ps = jnp.concatenate([ps_ref[...]] * (ts // LANES), axis=1)
    d1 = jnp.sum(jnp.where(sub == ids[0:1, :], ps, 0), axis=0, keepdims=True) + ids[2:3, :]
    d2 = jnp.sum(jnp.where(sub == ids[1:2, :], ps, 0), axis=0, keepdims=True) + ids[3:4, :]
    dst_ref[...] = jnp.concatenate([d1, d2, jnp.zeros((SUBLANES - 2, ts), jnp.int32)], axis=0)


def _dest(ids, pad_starts_rep):
    t = ids.shape[1]
    ts = min(2048, t)
    return pl.pallas_call(
        _dest_kernel,
        grid=(t // ts,),
        in_specs=[pl.BlockSpec((SUBLANES, ts), lambda i: (0, i)), _full(pad_starts_rep)],
        out_specs=pl.BlockSpec((SUBLANES, ts), lambda i: (0, i)),
        out_shape=jax.ShapeDtypeStruct((SUBLANES, t), jnp.int32),
        compiler_params=_cparams("arbitrary"),
        name="moe_dest",
    )(ids, pad_starts_rep)


def _hbm_row(ref, r):
    return ref.at[pl.ds(r, 1), :]


def _tile_row(ref3, g, u):
    return ref3.at[g, pl.ds(u, 1), :]


def _for_rows(n_rows, fn):
    def trip(g, c):
        for u in range(SUBLANES):
            fn(g, u)
        return c
    lax.fori_loop(0, n_rows // SUBLANES, trip, 0)


def _expert_kernel(be_ref, nu_ref, d1_ref, d2_ref, hf_ref, zeros_ref, wgu_ref, wdn_ref, y_ref, inv, xbuf, wgu_b, wdn_b,
                   sem, inv_sem):
    b = pl.program_id(0)
    n_tok = d1_ref.shape[0]
    d = y_ref.shape[1]
    de = wdn_ref.shape[0]
    nu = nu_ref[0]
    n_buf = xbuf.shape[0]
    ahead = n_buf - 1
    slot = b % n_buf

    def gather(block, s):
        base = block * MOE_BLOCK
        dst = xbuf.at[s]
        for r in range(MOE_BLOCK):
            pltpu.make_async_copy(_hbm_row(hf_ref, inv[base + r]), _tile_row(dst, r // SUBLANES, r % SUBLANES),
                                  sem.at[s]).start()

    def wait_rows(s):
        pltpu.make_async_copy(xbuf.at[(s + 1) % n_buf], xbuf.at[s], sem.at[s]).wait()

    def last(block):
        return jnp.minimum(block, nu - 1)

    @pl.when(b == 0)
    def _():
        clear = pltpu.make_async_copy(zeros_ref, inv, inv_sem)
        clear.start()
        clear.wait()

        def fill(i, c):
            for u in range(SUBLANES):
                tok = i * SUBLANES + u
                inv[d1_ref[tok]] = tok
                inv[d2_ref[tok]] = tok
            return c

        lax.fori_loop(0, n_tok // SUBLANES, fill, 0)
        for k in range(ahead):
            gather(last(k), k)

    @pl.when(b < nu)
    def _():
        @pl.when((b == 0) | (be_ref[b] != be_ref[jnp.maximum(b - 1, 0)]))
        def _():
            wgu_b[...] = wgu_ref[...].astype(BF16)
            wdn_b[...] = wdn_ref[...].astype(BF16)

        wait_rows(slot)
        x = xbuf[slot].reshape(MOE_BLOCK, d).astype(BF16)
        gather(last(b + ahead), (b + ahead) % n_buf)
        gu = jnp.dot(x, wgu_b[...], preferred_element_type=F32)
        act = _silu(gu[:, :de]) * gu[:, de:]
        y_ref[...] = _bdot(act, wdn_b[...])

        @pl.when(b == nu - 1)
        def _():
            for k in range(1, n_buf):
                wait_rows((b + k) % n_buf)

    @pl.when(b >= nu)
    def _():
        y_ref[...] = jnp.zeros_like(y_ref)


def _experts(block_e, n_used, d1, d2, hf, wgu, wdn, layer, n_slots):
    t, d = hf.shape
    n_blocks = n_slots // MOE_BLOCK

    def slab(w):
        return pl.BlockSpec((None, None) + w.shape[2:],
                            lambda b, be, nu, a1, a2: (layer, be[jnp.minimum(b, nu[0] - 1)], 0, 0))

    grid_spec = pltpu.PrefetchScalarGridSpec(
        num_scalar_prefetch=4,
        grid=(n_blocks,),
        in_specs=[pl.BlockSpec(memory_space=pl.ANY), pl.BlockSpec(memory_space=pl.ANY), slab(wgu), slab(wdn)],
        out_specs=pl.BlockSpec((MOE_BLOCK, d), lambda b, be, nu, a1, a2: (b, 0)),
        scratch_shapes=[pltpu.SMEM((n_slots,), jnp.int32),
                        pltpu.VMEM((EXPERT_ROW_BUFFERS, MOE_BLOCK // SUBLANES, SUBLANES, d), F32),
                        pltpu.VMEM(wgu.shape[2:], BF16), pltpu.VMEM(wdn.shape[2:], BF16),
                        pltpu.SemaphoreType.DMA((EXPERT_ROW_BUFFERS,)), pltpu.SemaphoreType.DMA],
    )
    return pl.pallas_call(
        _expert_kernel,
        grid_spec=grid_spec,
        out_shape=jax.ShapeDtypeStruct((n_slots, d), F32),
        compiler_params=_cparams("arbitrary"),
        name="moe_experts",
    )(block_e, n_used, d1, d2, hf, jnp.zeros((n_slots,), jnp.int32), wgu, wdn)


def _combine_kernel(d1_ref, d2_ref, x_ref, w1_ref, w2_ref, nf_ref, yb_ref, o_ref, buf1, buf2, sem, *, n_tiles):
    ts = x_ref.shape[0]
    i = pl.program_id(0)
    slot = i % 2

    def gather(tile, s):
        base = tile * ts
        b1, b2, sm = buf1.at[s], buf2.at[s], sem.at[s]

        def one(g, u):
            tok = base + g * SUBLANES + u
            pltpu.make_async_copy(_hbm_row(yb_ref, d1_ref[tok]), _tile_row(b1, g, u), sm).start()
            pltpu.make_async_copy(_hbm_row(yb_ref, d2_ref[tok]), _tile_row(b2, g, u), sm).start()

        _for_rows(ts, one)

    @pl.when(i == 0)
    def _():
        gather(0, 0)

    @pl.when(i + 1 < n_tiles)
    def _():
        gather(i + 1, 1 - slot)

    pltpu.make_async_copy(buf2.at[1 - slot], buf1.at[slot], sem.at[slot]).wait()
    pltpu.make_async_copy(buf1.at[1 - slot], buf2.at[slot], sem.at[slot]).wait()
    d = x_ref.shape[1]
    out = x_ref[...] + w1_ref[...] * buf1[slot].reshape(ts, d) + w2_ref[...] * buf2[slot].reshape(ts, d)
    o_ref[...] = _rms(out, nf_ref[...])


def _combine_final(d1, d2, x2, w1c, w2c, nf, yb):
    t, d = x2.shape
    ts = min(256, t)
    row = lambda i, a, b: (i, 0)
    grid_spec = pltpu.PrefetchScalarGridSpec(
        num_scalar_prefetch=2,
        grid=(t // ts,),
        in_specs=[pl.BlockSpec((ts, d), row), pl.BlockSpec((ts, 1), row), pl.BlockSpec((ts, 1), row),
                  pl.BlockSpec((1, d), lambda i, a, b: (0, 0)), pl.BlockSpec(memory_space=pl.ANY)],
        out_specs=pl.BlockSpec((ts, d), row),
        scratch_shapes=[pltpu.VMEM((2, ts // SUBLANES, SUBLANES, d), F32),
                        pltpu.VMEM((2, ts // SUBLANES, SUBLANES, d), F32), pltpu.SemaphoreType.DMA((2,))],
    )
    return pl.pallas_call(
        functools.partial(_combine_kernel, n_tiles=t // ts),
        grid_spec=grid_spec,
        out_shape=jax.ShapeDtypeStruct((t, d), F32),
        compiler_params=_cparams("arbitrary"),
        name="moe_combine",
    )(d1, d2, x2, w1c, w2c, nf, yb)


def _moe_experts(hf, logits_t, wgu, wdn, layer):
    t, d = hf.shape
    ids, wts, counts = _route(logits_t)
    counts = counts[:, 0]
    padded = (counts + MOE_BLOCK - 1) // MOE_BLOCK * MOE_BLOCK
    pad_ends = jnp.cumsum(padded)
    pad_starts = pad_ends - padded
    n_blocks = (2 * t + MOE_BLOCK - 1) // MOE_BLOCK + N_EXPERTS
    block_start = jnp.arange(n_blocks, dtype=jnp.int32) * MOE_BLOCK
    block_e = jnp.minimum(jnp.sum(pad_ends[None, :] <= block_start[:, None], axis=1), N_EXPERTS - 1).astype(jnp.int32)
    n_used = (pad_ends[-1:] // MOE_BLOCK).astype(jnp.int32)
    dst = _dest(ids, jnp.broadcast_to(pad_starts[:, None].astype(jnp.int32), (N_EXPERTS, LANES)))
    d1, d2 = dst[0], dst[1]
    yb = _experts(block_e, n_used, d1, d2, hf, wgu, wdn, layer, n_blocks * MOE_BLOCK)
    return d1, d2, wts[0].reshape(t, 1), wts[1].reshape(t, 1), yb


def _pair_blockdiag(w):
    z = jnp.zeros_like(w[0::2])
    top = jnp.concatenate([w[0::2], z], axis=2)
    bot = jnp.concatenate([z, w[1::2]], axis=2)
    return jnp.concatenate([top, bot], axis=1)


def kernel(x, mem, norm_mix, w_in, conv_a_w, conv_a_b, lru_gx_w, lru_gx_b, lru_ga_w, lru_ga_b, lru_lambda, sgu_norm, sgu_w, sgu_b, conv_c_w, conv_c_b, ssm_dt_bias, ssm_a_log, ssm_d, ssm_norm, w_a_proj, w_b_proj, w_c_proj, w_out, norm_xa, norm_mem, w_q, w_kv, w_o, norm_ffn, w_router_group, b_router_group, w_router_expert, b_router_expert, w_gate_up, w_down, norm_final):
    bsz, seq, d = x.shape
    mem_len = mem.shape[1]
    depth = w_in.shape[0]
    lru_w = conv_a_w.shape[2]
    sgu_wd = sgu_norm.shape[1]
    ssm_w = ssm_norm.shape[1]
    conv_dim = conv_c_w.shape[2]
    off_ya = lru_w
    off_uv = 2 * lru_w
    off_z = off_uv + 2 * sgu_wd
    off_xbc = off_z + ssm_w
    off_dt = off_xbc + conv_dim
    off_gate = off_dt + SSM_HEADS
    t = bsz * seq
    row = lambda v: v.reshape(1, -1).astype(F32)

    expand = (jnp.arange(LANES)[:, None] == (jnp.arange(ssm_w)[None, :] // SSM_HEAD_DIM)).astype(BF16)

    xcur = x.reshape(t, d)
    mem2d = mem.reshape(bsz * mem_len, d)
    pending = None
    for l in range(depth):
        win = w_in[l].astype(BF16)
        nrm = row(norm_mix[l])
        gates = win[:, off_gate:]
        gw = jnp.concatenate([_pair_blockdiag(lru_gx_w[l]), _pair_blockdiag(lru_ga_w[l])], axis=2).astype(BF16)
        sgu_bias = jnp.repeat(sgu_b[l].T, sgu_wd // SGU_GROUPS, axis=1)
        m_b = _sgu_branch(xcur, seq, nrm, _Cols(win, off_uv, sgu_wd), _Cols(win, off_uv + sgu_wd, sgu_wd),
                          _Cols(gates, d, d), row(sgu_norm[l]), sgu_w[l], sgu_bias, w_b_proj[l].astype(BF16),
                          pending=pending)
        if pending is not None:
            m_b, xcur = m_b
        m_a = _lru_branch(xcur, bsz, seq, nrm, _Cols(win, 0, lru_w), _Cols(win, off_ya, lru_w), _Cols(gates, 0, d),
                          conv_a_w[l], row(conv_a_b[l]), gw, row(lru_gx_b[l]), row(lru_ga_b[l]),
                          row(lru_lambda[l]), w_a_proj[l].astype(BF16))
        pad16 = lambda v: jnp.pad(v.reshape(1, -1), ((0, 0), (0, LANES - SSM_HEADS)))
        m_c = _ssd_branch(xcur, bsz, seq, nrm, _Cols(win, off_z, ssm_w), _Cols(win, off_xbc, ssm_w),
                          _Cols(win, off_xbc + ssm_w, conv_dim - ssm_w), _Cols(win, off_dt, LANES),
                          _Cols(gates, 2 * d, d), conv_c_w[l], row(conv_c_b[l]), pad16(ssm_dt_bias[l]),
                          pad16(ssm_a_log[l]), row(jnp.repeat(ssm_d[l], SSM_HEAD_DIM)), expand,
                          row(ssm_norm[l]), w_c_proj[l].astype(BF16))
        k2d, v2d = _kv_proj(mem2d, row(norm_mem[l]), w_kv[l].astype(BF16))
        n_r = N_EXPERT_GROUPS + N_EXPERTS
        wr_t = jnp.pad(jnp.concatenate([w_router_group[l], w_router_expert[l]], axis=1).T,
                       ((0, LANES - n_r), (0, 0))).astype(BF16)
        br_col = jnp.pad(jnp.concatenate([b_router_group[l], b_router_expert[l]]), (0, LANES - n_r)).reshape(LANES, 1)
        x2, hf, logits_t = _attn_block(xcur, m_a, m_b, m_c, k2d, v2d, bsz, seq, mem_len, w_out[l].astype(BF16),
                                       row(norm_xa[l]), w_q[l].astype(BF16), w_o[l].astype(BF16),
                                       row(norm_ffn[l]), wr_t, br_col)
        xcur = x2
        pending = _moe_experts(hf, logits_t, w_gate_up, w_down, l)
    d1, d2, w1c, w2c, yb = pending
    return _combine_final(d1, d2, xcur, w1c, w2c, row(norm_final), yb).reshape(bsz, seq, d)
```

```python
import functools
from typing import NamedTuple

import jax
import jax.numpy as jnp
from jax import lax
from jax.experimental import pallas as pl
from jax.experimental.pallas import tpu as pltpu

F32 = jnp.float32
BF16 = jnp.bfloat16

EPS = 1e-6
TINY = float(jnp.finfo(jnp.float32).tiny)
CONV_WIDTH = 4
CONV_PAD = 8
LRU_HEADS = 8
LRU_C = 8.0
SGU_GROUPS = 4
SGU_BLOCK = 128
SGU_CHUNK = 64
SSM_HEADS = 16
SSM_HEAD_DIM = 64
SSM_GROUPS = 4
SSM_STATE = 128
SSD_CHUNK = 128
XA_HEADS = 4
N_EXPERT_GROUPS = 4
EXPERTS_PER_GROUP = 8
N_EXPERTS = N_EXPERT_GROUPS * EXPERTS_PER_GROUP
MOE_BLOCK = 512
DISPATCH_SLOTS = 3
COMBINE_ROW_BUFFERS = 3
LANES = 128
SUBLANES = 8
VMEM_LIMIT_BYTES = 56 * 1024 * 1024


def _cparams(*sem):
    return pltpu.CompilerParams(dimension_semantics=sem, vmem_limit_bytes=VMEM_LIMIT_BYTES)


def _full(a):
    nd = a.ndim
    return pl.BlockSpec(a.shape, lambda *_: (0,) * nd)


class _Cols(NamedTuple):
    arr: jax.Array
    off: int
    width: int


def _spec(c):
    if isinstance(c, _Cols):
        blk, rem = divmod(c.off, c.width)
        assert rem == 0
        return pl.BlockSpec((c.arr.shape[0], c.width), lambda *_: (0, blk))
    return _full(c)


def _arr(c):
    return c.arr if isinstance(c, _Cols) else c


def _bdot(a, b):
    return jnp.dot(a.astype(BF16), b.astype(BF16), preferred_element_type=F32)


def _bdot_nt(a, b):
    return lax.dot_general(a.astype(BF16), b.astype(BF16), (((1,), (1,)), ((), ())),
                           preferred_element_type=F32)


def _split3(v):
    hi = v.astype(BF16)
    r1 = v - hi.astype(F32)
    mid = r1.astype(BF16)
    lo = (r1 - mid.astype(F32)).astype(BF16)
    return hi, mid, lo


def _dot_exact_rhs(v, m01):
    hi, mid, lo = _split3(v)
    d = functools.partial(jnp.dot, preferred_element_type=F32)
    return d(hi, m01) + d(mid, m01) + d(lo, m01)


def _dot_exact_lhs(m01, v):
    hi, mid, lo = _split3(v)
    d = functools.partial(jnp.dot, preferred_element_type=F32)
    return d(m01, hi) + d(m01, mid) + d(m01, lo)


def _rms(x, g):
    return x * lax.rsqrt(jnp.mean(x * x, axis=-1, keepdims=True) + EPS) * g


def _sigmoid(x):
    return 1.0 / (1.0 + jnp.exp(-x))


def _silu(x):
    return x * _sigmoid(x)


def _softplus(x):
    return jnp.maximum(x, 0.0) + jnp.log1p(jnp.exp(-jnp.abs(x)))


def _gelu(x):
    return jax.nn.gelu(x)


def _conv4(buf, tile_rows, xin, cw, cb):
    buf[pl.ds(CONV_PAD, tile_rows), :] = xin
    acc = cb
    for k in range(CONV_WIDTH):
        acc = acc + cw[k:k + 1, :] * buf[pl.ds(CONV_PAD - (CONV_WIDTH - 1) + k, tile_rows), :]
    buf[pl.ds(0, CONV_PAD), :] = buf[pl.ds(tile_rows, CONV_PAD), :]
    return acc


def _lru_kernel(x_ref, nrm_ref, wxa_ref, wya_ref, wg_ref, cw_ref, cb_ref, gw_ref, gxb_ref, gab_ref,
                lam_ref, wp_ref, o_ref, xa_buf, a_buf, b_buf, h_buf, carry):
    ts, width = a_buf.shape
    j = pl.program_id(1)

    @pl.when(j == 0)
    def _():
        xa_buf[pl.ds(0, CONV_PAD), :] = jnp.zeros((CONV_PAD, width), F32)
        carry[...] = jnp.zeros_like(carry)

    h = _rms(x_ref[...], nrm_ref[...]).astype(BF16)
    xa = jnp.dot(h, wxa_ref[...], preferred_element_type=F32)
    xc = _conv4(xa_buf, ts, xa, cw_ref[...], cb_ref[...])

    pair = 2 * (width // LRU_HEADS)
    gxs, gas = [], []
    for p in range(LRU_HEADS // 2):
        g = _bdot(xc[:, p * pair:(p + 1) * pair], gw_ref[p])
        gxs.append(g[:, :pair])
        gas.append(g[:, pair:])
    gate_x = _sigmoid(jnp.concatenate(gxs, axis=1) + gxb_ref[...])
    gate_a = _sigmoid(jnp.concatenate(gas, axis=1) + gab_ref[...])
    log_a = (-LRU_C * gate_a) * _softplus(-lam_ref[...])
    a = jnp.exp(log_a)
    m = jnp.maximum(1.0 - a * a, 0.0)
    mult = m * lax.rsqrt(jnp.maximum(m, TINY))
    b = xc * gate_x * mult

    a3 = a.reshape(ts // SUBLANES, SUBLANES, width)
    b3 = b.reshape(ts // SUBLANES, SUBLANES, width)
    sub = lax.broadcasted_iota(jnp.int32, (1, SUBLANES, width), 1)
    for d in (1, 2, 4):
        keep = sub >= d
        a_sh = jnp.where(keep, pltpu.roll(a3, d, 1), 1.0)
        b_sh = jnp.where(keep, pltpu.roll(b3, d, 1), 0.0)
        b3 = a3 * b_sh + b3
        a3 = a3 * a_sh
    a_buf[...] = a3.reshape(ts, width)
    b_buf[...] = b3.reshape(ts, width)

    def group(gi, hc):
        rows = pl.ds(pl.multiple_of(gi * SUBLANES, SUBLANES), SUBLANES)
        hg = b_buf[rows, :] + a_buf[rows, :] * hc
        h_buf[rows, :] = hg
        return jnp.broadcast_to(hg[SUBLANES - 1:SUBLANES, :], (SUBLANES, width))

    carry[...] = lax.fori_loop(0, ts // SUBLANES, group, carry[...])

    ya = jnp.dot(h, wya_ref[...], preferred_element_type=F32)
    out = h_buf[...] * _gelu(ya)
    y = _bdot(out, wp_ref[...])
    gate = _sigmoid(jnp.dot(h, wg_ref[...], preferred_element_type=F32))
    o_ref[...] = (gate * y).astype(o_ref.dtype)


def _lru_branch(x2d, bsz, seq, nrm, wxa, wya, wg, cw, cb, gw, gxb, gab, lam, wp):
    t, d = x2d.shape
    width = cw.shape[1]
    ts = min(512, seq)
    nj = seq // ts
    row = lambda b, j: (b * nj + j, 0)
    consts = (nrm, wxa, wya, wg, cw, cb, gw, gxb, gab, lam, wp)
    return pl.pallas_call(
        _lru_kernel,
        grid=(bsz, nj),
        in_specs=[pl.BlockSpec((ts, d), row)] + [_spec(c) for c in consts],
        out_specs=pl.BlockSpec((ts, d), row),
        out_shape=jax.ShapeDtypeStruct((t, d), BF16),
        scratch_shapes=[pltpu.VMEM((ts + CONV_PAD, width), F32), pltpu.VMEM((ts, width), F32),
                        pltpu.VMEM((ts, width), F32), pltpu.VMEM((ts, width), F32),
                        pltpu.VMEM((SUBLANES, width), F32)],
        compiler_params=_cparams("arbitrary", "arbitrary"),
        name="lru_branch",
    )(x2d, *[_arr(c) for c in consts])


def _sgu_kernel(x_ref, *rest):
    _sgu_body(x_ref[...], *rest)


def _sgu_combine_kernel(d1_ref, d2_ref, x_ref, w1_ref, w2_ref, yb_ref, *rest, n_tiles):
    *sgu_refs, x3_ref, mix_buf, buf1, buf2, sem = rest
    ts, d = x_ref.shape
    i = pl.program_id(0)
    n_buf = buf1.shape[0]
    ahead = n_buf - 1
    slot = i % n_buf

    def row_copies(tile, s, g, u):
        tok = tile * ts + g * SUBLANES + u
        pltpu.make_async_copy(_hbm_row(yb_ref, d1_ref[tok]), _tile_row(buf1.at[s], g, u), sem.at[s]).start()
        pltpu.make_async_copy(_hbm_row(yb_ref, d2_ref[tok]), _tile_row(buf2.at[s], g, u), sem.at[s]).start()

    def wait_rows(s):
        other = (s + 1) % n_buf
        pltpu.make_async_copy(buf2.at[other], buf1.at[s], sem.at[s]).wait()
        pltpu.make_async_copy(buf1.at[other], buf2.at[s], sem.at[s]).wait()

    def last(tile):
        return jnp.minimum(tile, n_tiles - 1)

    @pl.when(i == 0)
    def _():
        for k in range(ahead):
            _for_rows(ts, functools.partial(row_copies, last(k), k))

    wait_rows(slot)
    x = x_ref[...] + w1_ref[...] * buf1[slot].reshape(ts, d) + w2_ref[...] * buf2[slot].reshape(ts, d)
    x3_ref[...] = x
    nxt, nxt_slot = last(i + ahead), (i + ahead) % n_buf
    for r in range(ts):
        row_copies(nxt, nxt_slot, r // SUBLANES, r % SUBLANES)
    _sgu_body(x, *sgu_refs, mix_buf)

    @pl.when(i == n_tiles - 1)
    def _():
        for k in range(1, n_buf):
            wait_rows((i + k) % n_buf)


def _sgu_body(x, nrm_ref, wu_ref, wv_ref, wg_ref, ng_ref, ws_ref, bias_ref, wp_ref, o_ref, mix_buf):
    ts, width = mix_buf.shape
    gdim = width // SGU_GROUPS
    h = _rms(x, nrm_ref[...]).astype(BF16)
    u = _gelu(jnp.dot(h, wu_ref[...], preferred_element_type=F32))
    v = _gelu(jnp.dot(h, wv_ref[...], preferred_element_type=F32))
    vc = v - jnp.mean(v, axis=-1, keepdims=True)
    v = (vc * lax.rsqrt(jnp.mean(vc * vc, axis=-1, keepdims=True) + EPS) * ng_ref[...]).astype(BF16)

    ti = lax.broadcasted_iota(jnp.int32, (SGU_BLOCK, SGU_BLOCK), 0) // SGU_CHUNK
    si = lax.broadcasted_iota(jnp.int32, (SGU_BLOCK, SGU_BLOCK), 1) // SGU_CHUNK
    mask = si <= ti
    wm = [jnp.where(mask, ws_ref[g], 0.0).astype(BF16) for g in range(SGU_GROUPS)]
    for blk in range(ts // SGU_BLOCK):
        rows = slice(blk * SGU_BLOCK, (blk + 1) * SGU_BLOCK)
        for g in range(SGU_GROUPS):
            cols = slice(g * gdim, (g + 1) * gdim)
            mix_buf[rows, cols] = jnp.dot(wm[g], v[rows, cols], preferred_element_type=F32) + bias_ref[:, cols]
    out = u * mix_buf[...]
    y = _bdot(out, wp_ref[...])
    gate = _sigmoid(jnp.dot(h, wg_ref[...], preferred_element_type=F32))
    o_ref[...] = (gate * y).astype(o_ref.dtype)


def _sgu_branch(x2d, seq, nrm, wu, wv, wg, ng, ws, bias, wp, pending=None):
    t, d = x2d.shape
    width = ng.shape[1]
    ts = min(512, seq)
    consts = (nrm, wu, wv, wg, ng, ws, bias, wp)
    if pending is not None:
        d1, d2, w1c, w2c, yb = pending
        row = lambda i, a, b: (i, 0)
        grid_spec = pltpu.PrefetchScalarGridSpec(
            num_scalar_prefetch=2,
            grid=(t // ts,),
            in_specs=[pl.BlockSpec((ts, d), row), pl.BlockSpec((ts, 1), row), pl.BlockSpec((ts, 1), row),
                      pl.BlockSpec(memory_space=pl.ANY)] + [_spec(c) for c in consts],
            out_specs=[pl.BlockSpec((ts, d), row), pl.BlockSpec((ts, d), row)],
            scratch_shapes=[pltpu.VMEM((ts, width), F32),
                            pltpu.VMEM((COMBINE_ROW_BUFFERS, ts // SUBLANES, SUBLANES, d), F32),
                            pltpu.VMEM((COMBINE_ROW_BUFFERS, ts // SUBLANES, SUBLANES, d), F32),
                            pltpu.SemaphoreType.DMA((COMBINE_ROW_BUFFERS,))],
        )
        return pl.pallas_call(
            functools.partial(_sgu_combine_kernel, n_tiles=t // ts),
            grid_spec=grid_spec,
            out_shape=[jax.ShapeDtypeStruct((t, d), BF16), jax.ShapeDtypeStruct((t, d), F32)],
            compiler_params=_cparams("arbitrary"),
            name="sgu_branch_combine",
        )(d1, d2, x2d, w1c, w2c, yb, *[_arr(c) for c in consts])
    return pl.pallas_call(
        _sgu_kernel,
        grid=(t // ts,),
        in_specs=[pl.BlockSpec((ts, d), lambda i: (i, 0))] + [_spec(c) for c in consts],
        out_specs=pl.BlockSpec((ts, d), lambda i: (i, 0)),
        out_shape=jax.ShapeDtypeStruct((t, d), BF16),
        scratch_shapes=[pltpu.VMEM((ts, width), F32)],
        compiler_params=_cparams("arbitrary"),
        name="sgu_branch",
    )(x2d, *[_arr(c) for c in consts])


def _ssd_kernel(x_ref, nrm_ref, wz_ref, wx_ref, wbc_ref, wdt_ref, wg_ref, cw_ref, cb_ref, dtb_ref, alog_ref,
                dskip_ref, expand_ref, ng_ref, wp_ref, o_ref, xbc_buf, y_buf, state):
    ts, width = y_buf.shape
    lc = SSD_CHUNK
    n_state = SSM_STATE
    gw = width // SSM_GROUPS
    hpg = SSM_HEADS // SSM_GROUPS
    p_dim = SSM_HEAD_DIM
    j = pl.program_id(1)

    @pl.when(j == 0)
    def _():
        xbc_buf[pl.ds(0, CONV_PAD), :] = jnp.zeros((CONV_PAD, xbc_buf.shape[1]), F32)
        state[...] = jnp.zeros_like(state)

    h = _rms(x_ref[...], nrm_ref[...]).astype(BF16)
    xbc = jnp.concatenate([jnp.dot(h, wx_ref[...], preferred_element_type=F32),
                           jnp.dot(h, wbc_ref[...], preferred_element_type=F32)], axis=1)
    xc = _silu(_conv4(xbc_buf, ts, xbc, cw_ref[...], cb_ref[...]))
    lane = lax.broadcasted_iota(jnp.int32, (1, LANES), 1)
    heads = lane < SSM_HEADS
    dt = jnp.where(heads, _softplus(jnp.dot(h, wdt_ref[...], preferred_element_type=F32) + dtb_ref[...]), 0.0)
    a_row = jnp.where(heads, -jnp.exp(alog_ref[...]), 0.0)
    da = dt * a_row

    li = lax.broadcasted_iota(jnp.int32, (lc, lc), 0)
    si = lax.broadcasted_iota(jnp.int32, (lc, lc), 1)
    causal = li >= si
    tri = causal.astype(BF16)
    plane = lax.broadcasted_iota(jnp.int32, (lc, 2 * p_dim), 1)
    expand = expand_ref[...]

    for c in range(ts // lc):
        rows = slice(c * lc, (c + 1) * lc)
        dt_c = dt[rows]
        cs = _dot_exact_lhs(tri, da[rows])
        cs_t = cs.T
        dt_t = dt_c.T
        cs_last = cs[lc - 1:lc, :]
        wst_e = _bdot(jnp.exp(cs_last - cs) * dt_c, expand)
        ecs_e = _bdot(jnp.exp(cs), expand)
        cd_e = _dot_exact_rhs(jnp.broadcast_to(jnp.exp(cs_last), (SUBLANES, LANES)), expand)[0:1, :]
        xs_c = xc[rows, :width]
        xw = (xs_c * wst_e).astype(BF16)
        xs_b = xs_c.astype(BF16)
        for g in range(SSM_GROUPS):
            bm = xc[rows, width + g * n_state: width + (g + 1) * n_state]
            cm = xc[rows, width + (SSM_GROUPS + g) * n_state: width + (SSM_GROUPS + g + 1) * n_state].astype(BF16)
            cb = _bdot_nt(cm, bm)
            st = state[g]
            y_g = jnp.dot(cm, st.astype(BF16), preferred_element_type=F32) * ecs_e[:, g * gw:(g + 1) * gw]
            yds = []
            for pr in range(hpg // 2):
                ws = []
                for r in (2 * pr, 2 * pr + 1):
                    hh = g * hpg + r
                    seg = cs[:, hh:hh + 1] - cs_t[hh:hh + 1, :]
                    ws.append((cb * jnp.exp(jnp.where(causal, seg, -jnp.inf)) * dt_t[hh:hh + 1, :]).astype(BF16))
                xp = xs_b[:, g * gw + pr * 2 * p_dim: g * gw + (pr + 1) * 2 * p_dim]
                zero = jnp.zeros_like(xp)
                rhs = jnp.concatenate([jnp.where(plane < p_dim, xp, zero), jnp.where(plane >= p_dim, xp, zero)], axis=0)
                yds.append(jnp.dot(jnp.concatenate(ws, axis=1), rhs, preferred_element_type=F32))
            y_buf[rows, g * gw:(g + 1) * gw] = y_g + jnp.concatenate(yds, axis=1)
            new = jnp.dot(bm.T.astype(BF16), xw[:, g * gw:(g + 1) * gw], preferred_element_type=F32)
            state[g] = st * cd_e[:, g * gw:(g + 1) * gw] + new

    z = jnp.dot(h, wz_ref[...], preferred_element_type=F32)
    y = y_buf[...] + xc[:, :width] * dskip_ref[...]
    yg = y * _silu(z)
    ng = ng_ref[...]
    parts = []
    for g in range(SSM_GROUPS):
        seg = yg[:, g * gw:(g + 1) * gw]
        parts.append(seg * lax.rsqrt(jnp.mean(seg * seg, axis=-1, keepdims=True) + EPS) * ng[:, g * gw:(g + 1) * gw])
    yn = jnp.concatenate(parts, axis=1)
    out = _bdot(yn, wp_ref[...])
    gate = _sigmoid(jnp.dot(h, wg_ref[...], preferred_element_type=F32))
    o_ref[...] = (gate * out).astype(o_ref.dtype)


def _ssd_branch(x2d, bsz, seq, nrm, wz, wx, wbc, wdt, wg, cw, cb, dtb, alog, dskip, expand, ng, wp):
    t, d = x2d.shape
    width = ng.shape[1]
    ts = min(256, seq)
    nj = seq // ts
    row = lambda b, j: (b * nj + j, 0)
    consts = (nrm, wz, wx, wbc, wdt, wg, cw, cb, dtb, alog, dskip, expand, ng, wp)
    return pl.pallas_call(
        _ssd_kernel,
        grid=(bsz, nj),
        in_specs=[pl.BlockSpec((ts, d), row)] + [_spec(c) for c in consts],
        out_specs=pl.BlockSpec((ts, d), row),
        out_shape=jax.ShapeDtypeStruct((t, d), BF16),
        scratch_shapes=[pltpu.VMEM((ts + CONV_PAD, cw.shape[1]), F32), pltpu.VMEM((ts, width), F32),
                        pltpu.VMEM((SSM_GROUPS, SSM_STATE, width // SSM_GROUPS), F32)],
        compiler_params=_cparams("arbitrary", "arbitrary"),
        name="ssd_branch",
    )(x2d, *[_arr(c) for c in consts])


def _kv_kernel(m_ref, nrm_ref, w_ref, k_ref, v_ref):
    d = k_ref.shape[1]
    h = _rms(m_ref[...], nrm_ref[...]).astype(BF16)
    kv = jnp.dot(h, w_ref[...], preferred_element_type=F32)
    k_ref[...] = kv[:, :d].astype(k_ref.dtype)
    v_ref[...] = kv[:, d:].astype(v_ref.dtype)


def _kv_proj(mem2d, nrm, wkv):
    t, d = mem2d.shape
    ts = min(512, t)
    return pl.pallas_call(
        _kv_kernel,
        grid=(t // ts,),
        in_specs=[pl.BlockSpec((ts, d), lambda i: (i, 0)), _full(nrm), _full(wkv)],
        out_specs=[pl.BlockSpec((ts, d), lambda i: (i, 0))] * 2,
        out_shape=[jax.ShapeDtypeStruct((t, d), BF16)] * 2,
        compiler_params=_cparams("arbitrary"),
        name="kv_proj",
    )(mem2d, nrm, wkv)


def _attn_kernel(x_ref, ma_ref, mb_ref, mc_ref, k_ref, v_ref, wout_ref, nxa_ref, wq_ref, wo_ref, nffn_ref,
                 wr_ref, br_ref, x2_ref, hf_ref, lg_ref):
    d = x_ref.shape[1]
    hd = d // XA_HEADS
    merged = ma_ref[...].astype(F32) + mb_ref[...].astype(F32) + mc_ref[...].astype(F32)
    x1 = x_ref[...] + _bdot(merged, wout_ref[...])
    q = (_bdot(_rms(x1, nxa_ref[...]), wq_ref[...]) * (hd ** -0.5)).astype(BF16)
    outs = []
    for hh in range(XA_HEADS):
        cols = slice(hh * hd, (hh + 1) * hd)
        s = _bdot_nt(q[:, cols], k_ref[:, cols])
        s = jnp.exp(s - jnp.max(s, axis=-1, keepdims=True))
        p = s / jnp.sum(s, axis=-1, keepdims=True)
        outs.append(_bdot(p, v_ref[:, cols]))
    x2 = x1 + _bdot(jnp.concatenate(outs, axis=1), wo_ref[...])
    x2_ref[...] = x2
    hf = _rms(x2, nffn_ref[...])
    hf_ref[...] = hf
    lg_ref[...] = _bdot_nt(wr_ref[...], hf) + br_ref[...]


def _attn_block(x2d, ma, mb, mc, k2d, v2d, bsz, seq, mem_len, wout, nxa, wq, wo, nffn, wr_t, br_col):
    t, d = x2d.shape
    ts = min(512, seq)
    nj = seq // ts
    row = lambda i: (i, 0)
    memrow = lambda i: (i // nj, 0)
    consts = (wout, nxa, wq, wo, nffn, wr_t, br_col)
    return pl.pallas_call(
        _attn_kernel,
        grid=(t // ts,),
        in_specs=[pl.BlockSpec((ts, d), row)] * 4 + [pl.BlockSpec((mem_len, d), memrow)] * 2
                 + [_spec(c) for c in consts],
        out_specs=[pl.BlockSpec((ts, d), row), pl.BlockSpec((ts, d), row), pl.BlockSpec((LANES, ts), lambda i: (0, i))],
        out_shape=[jax.ShapeDtypeStruct((t, d), F32), jax.ShapeDtypeStruct((t, d), F32),
                   jax.ShapeDtypeStruct((LANES, t), F32)],
        compiler_params=_cparams("arbitrary"),
        name="attn_block",
    )(x2d, ma, mb, mc, k2d, v2d, *consts)


def _first_max(rows):
    best = rows[0]
    for r in rows[1:]:
        best = jnp.maximum(best, r)
    idx = jnp.full(best.shape, len(rows) - 1, jnp.int32)
    for i in range(len(rows) - 2, -1, -1):
        idx = jnp.where(rows[i] >= best, i, idx)
    return best, idx


def _route_kernel(lg_ref, ids_ref, wts_ref, cnt_ref, carry):
    ts = lg_ref.shape[1]
    i = pl.program_id(0)

    @pl.when(i == 0)
    def _():
        carry[...] = jnp.zeros_like(carry)

    lg = lg_ref[...]
    grows = [lg[g:g + 1, :] for g in range(N_EXPERT_GROUPS)]
    gmax, grp = _first_max(grows)
    denom = grows[0] * 0.0
    for r in grows:
        denom = denom + jnp.exp(r - gmax)
    p_grp = 1.0 / denom

    erows = []
    for e in range(EXPERTS_PER_GROUP):
        r_last = N_EXPERT_GROUPS + (N_EXPERT_GROUPS - 1) * EXPERTS_PER_GROUP + e
        sel = lg[r_last:r_last + 1, :]
        for g in range(N_EXPERT_GROUPS - 2, -1, -1):
            r0 = N_EXPERT_GROUPS + g * EXPERTS_PER_GROUP + e
            sel = jnp.where(grp == g, lg[r0:r0 + 1, :], sel)
        erows.append(sel)
    v1, i1 = _first_max(erows)
    rest = [jnp.where(i1 == e, -jnp.inf, erows[e]) for e in range(EXPERTS_PER_GROUP)]
    v2, i2 = _first_max(rest)
    tt = jnp.exp(v2 - v1)
    w1 = p_grp / (1.0 + tt)
    w2 = p_grp * tt / (1.0 + tt)
    e1 = grp * EXPERTS_PER_GROUP + i1
    e2 = grp * EXPERTS_PER_GROUP + i2

    sub = lax.broadcasted_iota(jnp.int32, (N_EXPERTS, ts), 0)
    oh1 = sub == e1
    oh2 = sub == e2
    oh = jnp.where(oh1 | oh2, 1.0, 0.0)
    ti = lax.broadcasted_iota(jnp.int32, (ts, ts), 0)
    tj = lax.broadcasted_iota(jnp.int32, (ts, ts), 1)
    before = (ti < tj).astype(BF16)
    base = carry[...]
    rank = jnp.dot(oh.astype(BF16), before, preferred_element_type=F32) \
        + jnp.concatenate([base] * (ts // LANES), axis=1)
    rank1 = jnp.sum(jnp.where(oh1, rank, 0.0), axis=0, keepdims=True)
    rank2 = jnp.sum(jnp.where(oh2, rank, 0.0), axis=0, keepdims=True)
    carry[...] = base + jnp.sum(oh, axis=1, keepdims=True)

    zi = jnp.zeros((SUBLANES - 4, ts), jnp.int32)
    ids_ref[...] = jnp.concatenate([e1, e2, rank1.astype(jnp.int32), rank2.astype(jnp.int32), zi], axis=0)
    wts_ref[...] = jnp.concatenate([w1, w2, jnp.zeros((SUBLANES - 2, ts), F32)], axis=0)
    cnt_ref[...] = carry[...].astype(jnp.int32)


def _route(logits_t):
    t = logits_t.shape[1]
    ts = min(512, t)
    return pl.pallas_call(
        _route_kernel,
        grid=(t // ts,),
        in_specs=[pl.BlockSpec((LANES, ts), lambda i: (0, i))],
        out_specs=[pl.BlockSpec((SUBLANES, ts), lambda i: (0, i)), pl.BlockSpec((SUBLANES, ts), lambda i: (0, i)),
                   pl.BlockSpec((N_EXPERTS, LANES), lambda i: (0, 0))],
        out_shape=[jax.ShapeDtypeStruct((SUBLANES, t), jnp.int32), jax.ShapeDtypeStruct((SUBLANES, t), F32),
                   jax.ShapeDtypeStruct((N_EXPERTS, LANES), jnp.int32)],
        scratch_shapes=[pltpu.VMEM((N_EXPERTS, LANES), F32)],
        compiler_params=_cparams("arbitrary"),
        name="moe_route",
    )(logits_t)


def _dest_kernel(ids_ref, ps_ref, dst_ref):
    ts = ids_ref.shape[1]
    ids = ids_ref[...]
    sub = lax.broadcasted_iota(jnp.int32, (N_EXPERTS, ts), 0)
    ps = jnp.concatenate([ps_ref[...]] * (ts // LANES), axis=1)
    d1 = jnp.sum(jnp.where(sub == ids[0:1, :], ps, 0), axis=0, keepdims=True) + ids[2:3, :]
    d2 = jnp.sum(jnp.where(sub == ids[1:2, :], ps, 0), axis=0, keepdims=True) + ids[3:4, :]
    dst_ref[...] = jnp.concatenate([d1, d2, jnp.zeros((SUBLANES - 2, ts), jnp.int32)], axis=0)


def _dest(ids, pad_starts_rep):
    t = ids.shape[1]
    ts = min(2048, t)
    return pl.pallas_call(
        _dest_kernel,
        grid=(t // ts,),
        in_specs=[pl.BlockSpec((SUBLANES, ts), lambda i: (0, i)), _full(pad_starts_rep)],
        out_specs=pl.BlockSpec((SUBLANES, ts), lambda i: (0, i)),
        out_shape=jax.ShapeDtypeStruct((SUBLANES, t), jnp.int32),
        compiler_params=_cparams("arbitrary"),
        name="moe_dest",
    )(ids, pad_starts_rep)


def _hbm_row(ref, r):
    return ref.at[pl.ds(r, 1), :]


def _tile_row(ref3, g, u):
    return ref3.at[g, pl.ds(u, 1), :]


def _for_rows(n_rows, fn):
    def trip(g, c):
        for u in range(SUBLANES):
            fn(g, u)
        return c
    lax.fori_loop(0, n_rows // SUBLANES, trip, 0)


def _dispatch_kernel(d1_ref, d2_ref, pe_ref, hf_ref, xs_ref, ring, zbuf, in_sem, out_sem, z_sem, *, n_tiles):
    tg = ring.shape[1]
    ts = tg * SUBLANES
    n_blocks = xs_ref.shape[0] // MOE_BLOCK
    i = pl.program_id(0)
    slot = i % DISPATCH_SLOTS

    def load(tile, s):
        return pltpu.make_async_copy(hf_ref.at[pl.ds(tile * tg, tg)], ring.at[s], in_sem.at[s])

    def drain(s):
        for _ in range(2):
            pltpu.make_async_copy(hf_ref.at[pl.ds(0, tg)], ring.at[s], out_sem.at[s]).wait()

    def zero_block(first_row):
        rows = pl.ds(pl.multiple_of(first_row, MOE_BLOCK), MOE_BLOCK)
        return pltpu.make_async_copy(zbuf, xs_ref.at[rows, :], z_sem)

    @pl.when(i == 0)
    def _():
        load(0, 0).start()
        zbuf[...] = jnp.zeros_like(zbuf)
        for start in (True, False):
            for e in range(N_EXPERTS):
                prev_end = pe_ref[e - 1] if e else 0

                @pl.when(pe_ref[e] > prev_end)
                def _():
                    cp = zero_block(pe_ref[e] - MOE_BLOCK)
                    cp.start() if start else cp.wait()
            for blk in range(n_blocks - N_EXPERTS, n_blocks):
                @pl.when(blk * MOE_BLOCK >= pe_ref[N_EXPERTS - 1])
                def _():
                    cp = zero_block(blk * MOE_BLOCK)
                    cp.start() if start else cp.wait()

    nxt = (i + 1) % DISPATCH_SLOTS

    @pl.when(i + 1 < n_tiles)
    def _():
        @pl.when(i + 1 >= DISPATCH_SLOTS)
        def _():
            drain(nxt)
        load(i + 1, nxt).start()

    load(i, slot).wait()
    base = i * ts
    src = ring.at[slot]
    sem = out_sem.at[slot]

    def scatter(g, u):
        tok = base + g * SUBLANES + u
        pltpu.make_async_copy(_tile_row(src, g, u), _hbm_row(xs_ref, d1_ref[tok]), sem).start()
        pltpu.make_async_copy(_tile_row(src, g, u), _hbm_row(xs_ref, d2_ref[tok]), sem).start()

    _for_rows(ts, scatter)

    @pl.when(i == n_tiles - 1)
    def _():
        for back in range(min(DISPATCH_SLOTS, n_tiles)):
            drain((n_tiles - 1 - back) % DISPATCH_SLOTS)


def _dispatch(d1, d2, pad_ends, hf, n_slots):
    t, d = hf.shape
    ts = min(256, t)
    n_tiles = t // ts
    hf = hf.reshape(t // SUBLANES, SUBLANES, d)
    grid_spec = pltpu.PrefetchScalarGridSpec(
        num_scalar_prefetch=3,
        grid=(n_tiles,),
        in_specs=[pl.BlockSpec(memory_space=pl.ANY)],
        out_specs=pl.BlockSpec(memory_space=pl.ANY),
        scratch_shapes=[pltpu.VMEM((DISPATCH_SLOTS, ts // SUBLANES, SUBLANES, d), F32), pltpu.VMEM((MOE_BLOCK, d), F32),
                        pltpu.SemaphoreType.DMA((DISPATCH_SLOTS,)), pltpu.SemaphoreType.DMA((DISPATCH_SLOTS,)),
                        pltpu.SemaphoreType.DMA],
    )
    return pl.pallas_call(
        functools.partial(_dispatch_kernel, n_tiles=n_tiles),
        grid_spec=grid_spec,
        out_shape=jax.ShapeDtypeStruct((n_slots, d), F32),
        compiler_params=_cparams("arbitrary"),
        name="moe_dispatch",
    )(d1, d2, pad_ends, hf)


def _expert_kernel(be_ref, nu_ref, xs_ref, wgu_ref, wdn_ref, y_ref, wgu_b, wdn_b):
    b = pl.program_id(0)
    de = wdn_ref.shape[0]

    @pl.when(b < nu_ref[0])
    def _():
        @pl.when((b == 0) | (be_ref[b] != be_ref[jnp.maximum(b - 1, 0)]))
        def _():
            wgu_b[...] = wgu_ref[...].astype(BF16)
            wdn_b[...] = wdn_ref[...].astype(BF16)

        gu = _bdot(xs_ref[...], wgu_b[...])
        act = _silu(gu[:, :de]) * gu[:, de:]
        y_ref[...] = _bdot(act, wdn_b[...])

    @pl.when(b >= nu_ref[0])
    def _():
        y_ref[...] = jnp.zeros_like(y_ref)


def _experts(block_e, n_used, xs, wgu, wdn, layer):
    n_slots, d = xs.shape
    n_blocks = n_slots // MOE_BLOCK

    def blk(b, be, nu):
        return jnp.minimum(b, nu[0] - 1)

    def slab(w):
        return pl.BlockSpec((None, None) + w.shape[2:], lambda b, be, nu: (layer, be[blk(b, be, nu)], 0, 0))

    grid_spec = pltpu.PrefetchScalarGridSpec(
        num_scalar_prefetch=2,
        grid=(n_blocks,),
        in_specs=[pl.BlockSpec((MOE_BLOCK, d), lambda b, be, nu: (blk(b, be, nu), 0)), slab(wgu), slab(wdn)],
        out_specs=pl.BlockSpec((MOE_BLOCK, d), lambda b, be, nu: (b, 0)),
        scratch_shapes=[pltpu.VMEM(wgu.shape[2:], BF16), pltpu.VMEM(wdn.shape[2:], BF16)],
    )
    return pl.pallas_call(
        _expert_kernel,
        grid_spec=grid_spec,
        out_shape=jax.ShapeDtypeStruct((n_slots, d), F32),
        compiler_params=_cparams("arbitrary"),
        name="moe_experts",
    )(block_e, n_used, xs, wgu, wdn)


def _combine_kernel(d1_ref, d2_ref, x_ref, w1_ref, w2_ref, nf_ref, yb_ref, o_ref, buf1, buf2, sem, *, n_tiles):
    ts = x_ref.shape[0]
    i = pl.program_id(0)
    slot = i % 2

    def gather(tile, s):
        base = tile * ts
        b1, b2, sm = buf1.at[s], buf2.at[s], sem.at[s]

        def one(g, u):
            tok = base + g * SUBLANES + u
            pltpu.make_async_copy(_hbm_row(yb_ref, d1_ref[tok]), _tile_row(b1, g, u), sm).start()
            pltpu.make_async_copy(_hbm_row(yb_ref, d2_ref[tok]), _tile_row(b2, g, u), sm).start()

        _for_rows(ts, one)

    @pl.when(i == 0)
    def _():
        gather(0, 0)

    @pl.when(i + 1 < n_tiles)
    def _():
        gather(i + 1, 1 - slot)

    pltpu.make_async_copy(buf2.at[1 - slot], buf1.at[slot], sem.at[slot]).wait()
    pltpu.make_async_copy(buf1.at[1 - slot], buf2.at[slot], sem.at[slot]).wait()
    d = x_ref.shape[1]
    out = x_ref[...] + w1_ref[...] * buf1[slot].reshape(ts, d) + w2_ref[...] * buf2[slot].reshape(ts, d)
    o_ref[...] = _rms(out, nf_ref[...])


def _combine_final(d1, d2, x2, w1c, w2c, nf, yb):
    t, d = x2.shape
    ts = min(256, t)
    row = lambda i, a, b: (i, 0)
    grid_spec = pltpu.PrefetchScalarGridSpec(
        num_scalar_prefetch=2,
        grid=(t // ts,),
        in_specs=[pl.BlockSpec((ts, d), row), pl.BlockSpec((ts, 1), row), pl.BlockSpec((ts, 1), row),
                  pl.BlockSpec((1, d), lambda i, a, b: (0, 0)), pl.BlockSpec(memory_space=pl.ANY)],
        out_specs=pl.BlockSpec((ts, d), row),
        scratch_shapes=[pltpu.VMEM((2, ts // SUBLANES, SUBLANES, d), F32),
                        pltpu.VMEM((2, ts // SUBLANES, SUBLANES, d), F32), pltpu.SemaphoreType.DMA((2,))],
    )
    return pl.pallas_call(
        functools.partial(_combine_kernel, n_tiles=t // ts),
        grid_spec=grid_spec,
        out_shape=jax.ShapeDtypeStruct((t, d), F32),
        compiler_params=_cparams("arbitrary"),
        name="moe_combine",
    )(d1, d2, x2, w1c, w2c, nf, yb)


def _moe_experts(hf, logits_t, wgu, wdn, layer):
    t, d = hf.shape
    ids, wts, counts = _route(logits_t)
    counts = counts[:, 0]
    padded = (counts + MOE_BLOCK - 1) // MOE_BLOCK * MOE_BLOCK
    pad_ends = jnp.cumsum(padded)
    pad_starts = pad_ends - padded
    n_blocks = (2 * t + MOE_BLOCK - 1) // MOE_BLOCK + N_EXPERTS
    block_start = jnp.arange(n_blocks, dtype=jnp.int32) * MOE_BLOCK
    block_e = jnp.minimum(jnp.sum(pad_ends[None, :] <= block_start[:, None], axis=1), N_EXPERTS - 1).astype(jnp.int32)
    n_used = (pad_ends[-1:] // MOE_BLOCK).astype(jnp.int32)
    dst = _dest(ids, jnp.broadcast_to(pad_starts[:, None].astype(jnp.int32), (N_EXPERTS, LANES)))
    d1, d2 = dst[0], dst[1]
    xs = _dispatch(d1, d2, pad_ends.astype(jnp.int32), hf, n_blocks * MOE_BLOCK)
    yb = _experts(block_e, n_used, xs, wgu, wdn, layer)
    return d1, d2, wts[0].reshape(t, 1), wts[1].reshape(t, 1), yb


def _pair_blockdiag(w):
    z = jnp.zeros_like(w[0::2])
    top = jnp.concatenate([w[0::2], z], axis=2)
    bot = jnp.concatenate([z, w[1::2]], axis=2)
    return jnp.concatenate([top, bot], axis=1)


def kernel(x, mem, norm_mix, w_in, conv_a_w, conv_a_b, lru_gx_w, lru_gx_b, lru_ga_w, lru_ga_b, lru_lambda, sgu_norm, sgu_w, sgu_b, conv_c_w, conv_c_b, ssm_dt_bias, ssm_a_log, ssm_d, ssm_norm, w_a_proj, w_b_proj, w_c_proj, w_out, norm_xa, norm_mem, w_q, w_kv, w_o, norm_ffn, w_router_group, b_router_group, w_router_expert, b_router_expert, w_gate_up, w_down, norm_final):
    bsz, seq, d = x.shape
    mem_len = mem.shape[1]
    depth = w_in.shape[0]
    lru_w = conv_a_w.shape[2]
    sgu_wd = sgu_norm.shape[1]
    ssm_w = ssm_norm.shape[1]
    conv_dim = conv_c_w.shape[2]
    off_ya = lru_w
    off_uv = 2 * lru_w
    off_z = off_uv + 2 * sgu_wd
    off_xbc = off_z + ssm_w
    off_dt = off_xbc + conv_dim
    off_gate = off_dt + SSM_HEADS
    t = bsz * seq
    row = lambda v: v.reshape(1, -1).astype(F32)

    expand = (jnp.arange(LANES)[:, None] == (jnp.arange(ssm_w)[None, :] // SSM_HEAD_DIM)).astype(BF16)

    xcur = x.reshape(t, d)
    mem2d = mem.reshape(bsz * mem_len, d)
    pending = None
    for l in range(depth):
        win = w_in[l].astype(BF16)
        nrm = row(norm_mix[l])
        gates = win[:, off_gate:]
        gw = jnp.concatenate([_pair_blockdiag(lru_gx_w[l]), _pair_blockdiag(lru_ga_w[l])], axis=2).astype(BF16)
        sgu_bias = jnp.repeat(sgu_b[l].T, sgu_wd // SGU_GROUPS, axis=1)
        m_b = _sgu_branch(xcur, seq, nrm, _Cols(win, off_uv, sgu_wd), _Cols(win, off_uv + sgu_wd, sgu_wd),
                          _Cols(gates, d, d), row(sgu_norm[l]), sgu_w[l], sgu_bias, w_b_proj[l].astype(BF16),
                          pending=pending)
        if pending is not None:
            m_b, xcur = m_b
        m_a = _lru_branch(xcur, bsz, seq, nrm, _Cols(win, 0, lru_w), _Cols(win, off_ya, lru_w), _Cols(gates, 0, d),
                          conv_a_w[l], row(conv_a_b[l]), gw, row(lru_gx_b[l]), row(lru_ga_b[l]),
                          row(lru_lambda[l]), w_a_proj[l].astype(BF16))
        pad16 = lambda v: jnp.pad(v.reshape(1, -1), ((0, 0), (0, LANES - SSM_HEADS)))
        m_c = _ssd_branch(xcur, bsz, seq, nrm, _Cols(win, off_z, ssm_w), _Cols(win, off_xbc, ssm_w),
                          _Cols(win, off_xbc + ssm_w, conv_dim - ssm_w), _Cols(win, off_dt, LANES),
                          _Cols(gates, 2 * d, d), conv_c_w[l], row(conv_c_b[l]), pad16(ssm_dt_bias[l]),
                          pad16(ssm_a_log[l]), row(jnp.repeat(ssm_d[l], SSM_HEAD_DIM)), expand,
                          row(ssm_norm[l]), w_c_proj[l].astype(BF16))
        k2d, v2d = _kv_proj(mem2d, row(norm_mem[l]), w_kv[l].astype(BF16))
        n_r = N_EXPERT_GROUPS + N_EXPERTS
        wr_t = jnp.pad(jnp.concatenate([w_router_group[l], w_router_expert[l]], axis=1).T,
                       ((0, LANES - n_r), (0, 0))).astype(BF16)
        br_col = jnp.pad(jnp.concatenate([b_router_group[l], b_router_expert[l]]), (0, LANES - n_r)).reshape(LANES, 1)
        x2, hf, logits_t = _attn_block(xcur, m_a, m_b, m_c, k2d, v2d, bsz, seq, mem_len, w_out[l].astype(BF16),
                                       row(norm_xa[l]), w_q[l].astype(BF16), w_o[l].astype(BF16),
                                       row(norm_ffn[l]), wr_t, br_col)
        xcur = x2
        pending = _moe_experts(hf, logits_t, w_gate_up, w_down, l)
    d1, d2, w1c, w2c, yb = pending
    return _combine_final(d1, d2, xcur, w1c, w2c, row(norm_final), yb).reshape(bsz, seq, d)
```

```python
import functools
from typing import NamedTuple

import jax
import jax.numpy as jnp
from jax import lax
from jax.experimental import pallas as pl
from jax.experimental.pallas import tpu as pltpu

F32 = jnp.float32
BF16 = jnp.bfloat16

EPS = 1e-6
TINY = float(jnp.finfo(jnp.float32).tiny)
CONV_WIDTH = 4
CONV_PAD = 8
LRU_HEADS = 8
LRU_C = 8.0
SGU_GROUPS = 4
SGU_BLOCK = 128
SGU_CHUNK = 64
SSM_HEADS = 16
SSM_HEAD_DIM = 64
SSM_GROUPS = 4
SSM_STATE = 128
SSD_CHUNK = 128
XA_HEADS = 4
N_EXPERT_GROUPS = 4
EXPERTS_PER_GROUP = 8
N_EXPERTS = N_EXPERT_GROUPS * EXPERTS_PER_GROUP
MOE_BLOCK = 512
DISPATCH_SLOTS = 3
COMBINE_ROW_BUFFERS = 3
LANES = 128
SUBLANES = 8
VMEM_LIMIT_BYTES = 56 * 1024 * 1024


def _cparams(*sem):
    return pltpu.CompilerParams(dimension_semantics=sem, vmem_limit_bytes=VMEM_LIMIT_BYTES)


def _full(a):
    nd = a.ndim
    return pl.BlockSpec(a.shape, lambda *_: (0,) * nd)


class _Cols(NamedTuple):
    arr: jax.Array
    off: int
    width: int


def _spec(c):
    if isinstance(c, _Cols):
        blk, rem = divmod(c.off, c.width)
        assert rem == 0
        return pl.BlockSpec((c.arr.shape[0], c.width), lambda *_: (0, blk))
    return _full(c)


def _arr(c):
    return c.arr if isinstance(c, _Cols) else c


def _bdot(a, b):
    return jnp.dot(a.astype(BF16), b.astype(BF16), preferred_element_type=F32)


def _bdot_nt(a, b):
    return lax.dot_general(a.astype(BF16), b.astype(BF16), (((1,), (1,)), ((), ())),
                           preferred_element_type=F32)


def _split3(v):
    hi = v.astype(BF16)
    r1 = v - hi.astype(F32)
    mid = r1.astype(BF16)
    lo = (r1 - mid.astype(F32)).astype(BF16)
    return hi, mid, lo


def _dot_exact_rhs(v, m01):
    hi, mid, lo = _split3(v)
    d = functools.partial(jnp.dot, preferred_element_type=F32)
    return d(hi, m01) + d(mid, m01) + d(lo, m01)


def _dot_exact_lhs(m01, v):
    hi, mid, lo = _split3(v)
    d = functools.partial(jnp.dot, preferred_element_type=F32)
    return d(m01, hi) + d(m01, mid) + d(m01, lo)


def _rms(x, g):
    return x * lax.rsqrt(jnp.mean(x * x, axis=-1, keepdims=True) + EPS) * g


def _sigmoid(x):
    return 1.0 / (1.0 + jnp.exp(-x))


def _silu(x):
    return x * _sigmoid(x)


def _softplus(x):
    return jnp.maximum(x, 0.0) + jnp.log1p(jnp.exp(-jnp.abs(x)))


def _gelu(x):
    return jax.nn.gelu(x)


def _conv4(buf, tile_rows, xin, cw, cb):
    buf[pl.ds(CONV_PAD, tile_rows), :] = xin
    acc = cb
    for k in range(CONV_WIDTH):
        acc = acc + cw[k:k + 1, :] * buf[pl.ds(CONV_PAD - (CONV_WIDTH - 1) + k, tile_rows), :]
    buf[pl.ds(0, CONV_PAD), :] = buf[pl.ds(tile_rows, CONV_PAD), :]
    return acc


def _to_phases(slab, v):
    n_slab, rows, _ = slab.shape
    for q in range(n_slab):
        slab[q] = v[:, q * LANES:(q + 1) * LANES]
    return [jnp.concatenate([slab[q, pl.ds(s, rows // SUBLANES, stride=SUBLANES), :] for q in range(n_slab)], axis=1)
            for s in range(SUBLANES)]


def _from_phases(slab, phases):
    n_slab, rows, _ = slab.shape
    for s, v in enumerate(phases):
        for q in range(n_slab):
            slab[q, pl.ds(s, rows // SUBLANES, stride=SUBLANES), :] = v[:, q * LANES:(q + 1) * LANES]
    return jnp.concatenate([slab[q] for q in range(n_slab)], axis=1)


def _phase_conv(xp, hist, cw, cb):
    ng, width = xp[0].shape
    taps = CONV_WIDTH - 1
    first_group = lax.broadcasted_iota(jnp.int32, (ng, width), 0) == 0
    back = [jnp.where(first_group, hist[i:i + 1, :], pltpu.roll(xp[SUBLANES - taps + i], 1, 0)) for i in range(taps)]
    src = back + xp
    out = []
    for s in range(SUBLANES):
        acc = cb
        for k in range(CONV_WIDTH):
            acc = acc + cw[k:k + 1, :] * src[s + k]
        out.append(acc)
    for i in range(taps):
        hist[i:i + 1, :] = xp[SUBLANES - taps + i][ng - 1:ng, :]
    return out


def _lru_kernel(x_ref, nrm_ref, wxa_ref, wya_ref, wg_ref, cw_ref, cb_ref, gw_ref, gxb_ref, gab_ref,
                lam_ref, wp_ref, o_ref, slab, p_buf, h_buf, c_buf, hist):
    n_slab, ts, _ = slab.shape
    width = n_slab * LANES
    ng = ts // SUBLANES
    taps = CONV_WIDTH - 1
    j = pl.program_id(1)

    @pl.when(j == 0)
    def _():
        hist[...] = jnp.zeros_like(hist)

    h = _rms(x_ref[...], nrm_ref[...]).astype(BF16)
    xp = _to_phases(slab, jnp.dot(h, wxa_ref[...], preferred_element_type=F32))
    xc = jnp.concatenate(_phase_conv(xp, hist, cw_ref[...], cb_ref[...]), axis=0)

    pair = 2 * (width // LRU_HEADS)
    gxs, gas = [], []
    for p in range(LRU_HEADS // 2):
        g = _bdot(xc[:, p * pair:(p + 1) * pair], gw_ref[p])
        gxs.append(g[:, :pair])
        gas.append(g[:, pair:])
    gate_x = _sigmoid(jnp.concatenate(gxs, axis=1) + gxb_ref[...])
    gate_a = _sigmoid(jnp.concatenate(gas, axis=1) + gab_ref[...])
    log_a = (-LRU_C * gate_a) * _softplus(-lam_ref[...])
    a = jnp.exp(log_a)
    m = jnp.maximum(1.0 - a * a, 0.0)
    mult = m * lax.rsqrt(jnp.maximum(m, TINY))
    b = xc * gate_x * mult

    hs = [b[:ng]]
    ps = [a[:ng]]
    for s in range(1, SUBLANES):
        a_s = a[s * ng:(s + 1) * ng]
        hs.append(a_s * hs[-1] + b[s * ng:(s + 1) * ng])
        ps.append(a_s * ps[-1])
    p_buf[...] = ps[-1]
    h_buf[...] = hs[-1]
    state = jnp.broadcast_to(hist[taps:taps + 1, :], (SUBLANES, width))
    for g in range(ng):
        c_buf[g:g + 1, :] = state[0:1, :]
        state = (jnp.broadcast_to(p_buf[g:g + 1, :], (SUBLANES, width)) * state
                 + jnp.broadcast_to(h_buf[g:g + 1, :], (SUBLANES, width)))
    hist[taps:taps + 1, :] = state[0:1, :]
    enter = c_buf[...]
    hn = _from_phases(slab, [hs[s] + ps[s] * enter for s in range(SUBLANES)])

    ya = jnp.dot(h, wya_ref[...], preferred_element_type=F32)
    out = hn * _gelu(ya)
    y = _bdot(out, wp_ref[...])
    gate = _sigmoid(jnp.dot(h, wg_ref[...], preferred_element_type=F32))
    o_ref[...] = (gate * y).astype(o_ref.dtype)


def _lru_branch(x2d, bsz, seq, nrm, wxa, wya, wg, cw, cb, gw, gxb, gab, lam, wp):
    t, d = x2d.shape
    width = cw.shape[1]
    ts = min(512, seq)
    nj = seq // ts
    row = lambda b, j: (b * nj + j, 0)
    consts = (nrm, wxa, wya, wg, cw, cb, gw, gxb, gab, lam, wp)
    return pl.pallas_call(
        _lru_kernel,
        grid=(bsz, nj),
        in_specs=[pl.BlockSpec((ts, d), row)] + [_spec(c) for c in consts],
        out_specs=pl.BlockSpec((ts, d), row),
        out_shape=jax.ShapeDtypeStruct((t, d), BF16),
        scratch_shapes=[pltpu.VMEM((width // LANES, ts, LANES), F32)]
                       + [pltpu.VMEM((ts // SUBLANES, width), F32)] * 3 + [pltpu.VMEM((SUBLANES, width), F32)],
        compiler_params=_cparams("arbitrary", "arbitrary"),
        name="lru_branch",
    )(x2d, *[_arr(c) for c in consts])


def _sgu_kernel(x_ref, *rest):
    _sgu_body(x_ref[...], *rest)


def _sgu_combine_kernel(d1_ref, d2_ref, x_ref, w1_ref, w2_ref, yb_ref, *rest, n_tiles):
    *sgu_refs, x3_ref, mix_buf, buf1, buf2, sem = rest
    ts, d = x_ref.shape
    i = pl.program_id(0)
    n_buf = buf1.shape[0]
    ahead = n_buf - 1
    slot = i % n_buf

    def row_copies(tile, s, g, u):
        tok = tile * ts + g * SUBLANES + u
        pltpu.make_async_copy(_hbm_row(yb_ref, d1_ref[tok]), _tile_row(buf1.at[s], g, u), sem.at[s]).start()
        pltpu.make_async_copy(_hbm_row(yb_ref, d2_ref[tok]), _tile_row(buf2.at[s], g, u), sem.at[s]).start()

    def wait_rows(s):
        other = (s + 1) % n_buf
        pltpu.make_async_copy(buf2.at[other], buf1.at[s], sem.at[s]).wait()
        pltpu.make_async_copy(buf1.at[other], buf2.at[s], sem.at[s]).wait()

    def last(tile):
        return jnp.minimum(tile, n_tiles - 1)

    @pl.when(i == 0)
    def _():
        for k in range(ahead):
            _for_rows(ts, functools.partial(row_copies, last(k), k))

    wait_rows(slot)
    x = x_ref[...] + w1_ref[...] * buf1[slot].reshape(ts, d) + w2_ref[...] * buf2[slot].reshape(ts, d)
    x3_ref[...] = x
    nxt, nxt_slot = last(i + ahead), (i + ahead) % n_buf
    for r in range(ts):
        row_copies(nxt, nxt_slot, r // SUBLANES, r % SUBLANES)
    _sgu_body(x, *sgu_refs, mix_buf)

    @pl.when(i == n_tiles - 1)
    def _():
        for k in range(1, n_buf):
            wait_rows((i + k) % n_buf)


def _sgu_body(x, nrm_ref, wu_ref, wv_ref, wg_ref, ng_ref, ws_ref, bias_ref, wp_ref, o_ref, mix_buf):
    ts, width = mix_buf.shape
    gdim = width // SGU_GROUPS
    h = _rms(x, nrm_ref[...]).astype(BF16)
    u = _gelu(jnp.dot(h, wu_ref[...], preferred_element_type=F32))
    v = _gelu(jnp.dot(h, wv_ref[...], preferred_element_type=F32))
    vc = v - jnp.mean(v, axis=-1, keepdims=True)
    v = (vc * lax.rsqrt(jnp.mean(vc * vc, axis=-1, keepdims=True) + EPS) * ng_ref[...]).astype(BF16)

    ti = lax.broadcasted_iota(jnp.int32, (SGU_BLOCK, SGU_BLOCK), 0) // SGU_CHUNK
    si = lax.broadcasted_iota(jnp.int32, (SGU_BLOCK, SGU_BLOCK), 1) // SGU_CHUNK
    mask = si <= ti
    wm = [jnp.where(mask, ws_ref[g], 0.0).astype(BF16) for g in range(SGU_GROUPS)]
    for blk in range(ts // SGU_BLOCK):
        rows = slice(blk * SGU_BLOCK, (blk + 1) * SGU_BLOCK)
        for g in range(SGU_GROUPS):
            cols = slice(g * gdim, (g + 1) * gdim)
            mix_buf[rows, cols] = jnp.dot(wm[g], v[rows, cols], preferred_element_type=F32) + bias_ref[:, cols]
    out = u * mix_buf[...]
    y = _bdot(out, wp_ref[...])
    gate = _sigmoid(jnp.dot(h, wg_ref[...], preferred_element_type=F32))
    o_ref[...] = (gate * y).astype(o_ref.dtype)


def _sgu_branch(x2d, seq, nrm, wu, wv, wg, ng, ws, bias, wp, pending=None):
    t, d = x2d.shape
    width = ng.shape[1]
    ts = min(512, seq)
    consts = (nrm, wu, wv, wg, ng, ws, bias, wp)
    if pending is not None:
        d1, d2, w1c, w2c, yb = pending
        row = lambda i, a, b: (i, 0)
        grid_spec = pltpu.PrefetchScalarGridSpec(
            num_scalar_prefetch=2,
            grid=(t // ts,),
            in_specs=[pl.BlockSpec((ts, d), row), pl.BlockSpec((ts, 1), row), pl.BlockSpec((ts, 1), row),
                      pl.BlockSpec(memory_space=pl.ANY)] + [_spec(c) for c in consts],
            out_specs=[pl.BlockSpec((ts, d), row), pl.BlockSpec((ts, d), row)],
            scratch_shapes=[pltpu.VMEM((ts, width), F32),
                            pltpu.VMEM((COMBINE_ROW_BUFFERS, ts // SUBLANES, SUBLANES, d), F32),
                            pltpu.VMEM((COMBINE_ROW_BUFFERS, ts // SUBLANES, SUBLANES, d), F32),
                            pltpu.SemaphoreType.DMA((COMBINE_ROW_BUFFERS,))],
        )
        return pl.pallas_call(
            functools.partial(_sgu_combine_kernel, n_tiles=t // ts),
            grid_spec=grid_spec,
            out_shape=[jax.ShapeDtypeStruct((t, d), BF16), jax.ShapeDtypeStruct((t, d), F32)],
            compiler_params=_cparams("arbitrary"),
            name="sgu_branch_combine",
        )(d1, d2, x2d, w1c, w2c, yb, *[_arr(c) for c in consts])
    return pl.pallas_call(
        _sgu_kernel,
        grid=(t // ts,),
        in_specs=[pl.BlockSpec((ts, d), lambda i: (i, 0))] + [_spec(c) for c in consts],
        out_specs=pl.BlockSpec((ts, d), lambda i: (i, 0)),
        out_shape=jax.ShapeDtypeStruct((t, d), BF16),
        scratch_shapes=[pltpu.VMEM((ts, width), F32)],
        compiler_params=_cparams("arbitrary"),
        name="sgu_branch",
    )(x2d, *[_arr(c) for c in consts])


def _ssd_kernel(x_ref, nrm_ref, wz_ref, wx_ref, wbc_ref, wdt_ref, wg_ref, cw_ref, cb_ref, dtb_ref, alog_ref,
                dskip_ref, expand_ref, ng_ref, wp_ref, o_ref, xbc_buf, y_buf, state):
    ts, width = y_buf.shape
    lc = SSD_CHUNK
    n_state = SSM_STATE
    gw = width // SSM_GROUPS
    hpg = SSM_HEADS // SSM_GROUPS
    p_dim = SSM_HEAD_DIM
    j = pl.program_id(1)

    @pl.when(j == 0)
    def _():
        xbc_buf[pl.ds(0, CONV_PAD), :] = jnp.zeros((CONV_PAD, xbc_buf.shape[1]), F32)
        state[...] = jnp.zeros_like(state)

    h = _rms(x_ref[...], nrm_ref[...]).astype(BF16)
    xbc = jnp.concatenate([jnp.dot(h, wx_ref[...], preferred_element_type=F32),
                           jnp.dot(h, wbc_ref[...], preferred_element_type=F32)], axis=1)
    xc = _silu(_conv4(xbc_buf, ts, xbc, cw_ref[...], cb_ref[...]))
    lane = lax.broadcasted_iota(jnp.int32, (1, LANES), 1)
    heads = lane < SSM_HEADS
    dt = jnp.where(heads, _softplus(jnp.dot(h, wdt_ref[...], preferred_element_type=F32) + dtb_ref[...]), 0.0)
    a_row = jnp.where(heads, -jnp.exp(alog_ref[...]), 0.0)
    da = dt * a_row

    li = lax.broadcasted_iota(jnp.int32, (lc, lc), 0)
    si = lax.broadcasted_iota(jnp.int32, (lc, lc), 1)
    causal = li >= si
    tri = causal.astype(BF16)
    plane = lax.broadcasted_iota(jnp.int32, (lc, 2 * p_dim), 1)
    expand = expand_ref[...]

    for c in range(ts // lc):
        rows = slice(c * lc, (c + 1) * lc)
        dt_c = dt[rows]
        cs = _dot_exact_lhs(tri, da[rows])
        cs_t = cs.T
        dt_t = dt_c.T
        cs_last = cs[lc - 1:lc, :]
        wst_e = _bdot(jnp.exp(cs_last - cs) * dt_c, expand)
        ecs_e = _bdot(jnp.exp(cs), expand)
        cd_e = _dot_exact_rhs(jnp.broadcast_to(jnp.exp(cs_last), (SUBLANES, LANES)), expand)[0:1, :]
        xs_c = xc[rows, :width]
        xw = (xs_c * wst_e).astype(BF16)
        xs_b = xs_c.astype(BF16)
        for g in range(SSM_GROUPS):
            bm = xc[rows, width + g * n_state: width + (g + 1) * n_state]
            cm = xc[rows, width + (SSM_GROUPS + g) * n_state: width + (SSM_GROUPS + g + 1) * n_state].astype(BF16)
            cb = _bdot_nt(cm, bm)
            st = state[g]
            y_g = jnp.dot(cm, st.astype(BF16), preferred_element_type=F32) * ecs_e[:, g * gw:(g + 1) * gw]
            yds = []
            for pr in range(hpg // 2):
                ws = []
                for r in (2 * pr, 2 * pr + 1):
                    hh = g * hpg + r
                    seg = cs[:, hh:hh + 1] - cs_t[hh:hh + 1, :]
                    ws.append((cb * jnp.exp(jnp.where(causal, seg, -jnp.inf)) * dt_t[hh:hh + 1, :]).astype(BF16))
                xp = xs_b[:, g * gw + pr * 2 * p_dim: g * gw + (pr + 1) * 2 * p_dim]
                zero = jnp.zeros_like(xp)
                rhs = jnp.concatenate([jnp.where(plane < p_dim, xp, zero), jnp.where(plane >= p_dim, xp, zero)], axis=0)
                yds.append(jnp.dot(jnp.concatenate(ws, axis=1), rhs, preferred_element_type=F32))
            y_buf[rows, g * gw:(g + 1) * gw] = y_g + jnp.concatenate(yds, axis=1)
            new = jnp.dot(bm.T.astype(BF16), xw[:, g * gw:(g + 1) * gw], preferred_element_type=F32)
            state[g] = st * cd_e[:, g * gw:(g + 1) * gw] + new

    z = jnp.dot(h, wz_ref[...], preferred_element_type=F32)
    y = y_buf[...] + xc[:, :width] * dskip_ref[...]
    yg = y * _silu(z)
    ng = ng_ref[...]
    parts = []
    for g in range(SSM_GROUPS):
        seg = yg[:, g * gw:(g + 1) * gw]
        parts.append(seg * lax.rsqrt(jnp.mean(seg * seg, axis=-1, keepdims=True) + EPS) * ng[:, g * gw:(g + 1) * gw])
    yn = jnp.concatenate(parts, axis=1)
    out = _bdot(yn, wp_ref[...])
    gate = _sigmoid(jnp.dot(h, wg_ref[...], preferred_element_type=F32))
    o_ref[...] = (gate * out).astype(o_ref.dtype)


def _ssd_branch(x2d, bsz, seq, nrm, wz, wx, wbc, wdt, wg, cw, cb, dtb, alog, dskip, expand, ng, wp):
    t, d = x2d.shape
    width = ng.shape[1]
    ts = min(512, seq)
    nj = seq // ts
    row = lambda b, j: (b * nj + j, 0)
    consts = (nrm, wz, wx, wbc, wdt, wg, cw, cb, dtb, alog, dskip, expand, ng, wp)
    return pl.pallas_call(
        _ssd_kernel,
        grid=(bsz, nj),
        in_specs=[pl.BlockSpec((ts, d), row)] + [_spec(c) for c in consts],
        out_specs=pl.BlockSpec((ts, d), row),
        out_shape=jax.ShapeDtypeStruct((t, d), BF16),
        scratch_shapes=[pltpu.VMEM((ts + CONV_PAD, cw.shape[1]), F32), pltpu.VMEM((ts, width), F32),
                        pltpu.VMEM((SSM_GROUPS, SSM_STATE, width // SSM_GROUPS), F32)],
        compiler_params=_cparams("arbitrary", "arbitrary"),
        name="ssd_branch",
    )(x2d, *[_arr(c) for c in consts])


def _kv_kernel(m_ref, nrm_ref, w_ref, k_ref, v_ref):
    d = k_ref.shape[1]
    h = _rms(m_ref[...], nrm_ref[...]).astype(BF16)
    kv = jnp.dot(h, w_ref[...], preferred_element_type=F32)
    k_ref[...] = kv[:, :d].astype(k_ref.dtype)
    v_ref[...] = kv[:, d:].astype(v_ref.dtype)


def _kv_proj(mem2d, nrm, wkv):
    t, d = mem2d.shape
    ts = min(512, t)
    return pl.pallas_call(
        _kv_kernel,
        grid=(t // ts,),
        in_specs=[pl.BlockSpec((ts, d), lambda i: (i, 0)), _full(nrm), _full(wkv)],
        out_specs=[pl.BlockSpec((ts, d), lambda i: (i, 0))] * 2,
        out_shape=[jax.ShapeDtypeStruct((t, d), BF16)] * 2,
        compiler_params=_cparams("arbitrary"),
        name="kv_proj",
    )(mem2d, nrm, wkv)


def _attn_kernel(x_ref, ma_ref, mb_ref, mc_ref, k_ref, v_ref, wout_ref, nxa_ref, wq_ref, wo_ref, nffn_ref,
                 wr_ref, br_ref, x2_ref, hf_ref, lg_ref):
    d = x_ref.shape[1]
    hd = d // XA_HEADS
    merged = ma_ref[...].astype(F32) + mb_ref[...].astype(F32) + mc_ref[...].astype(F32)
    x1 = x_ref[...] + _bdot(merged, wout_ref[...])
    q = (_bdot(_rms(x1, nxa_ref[...]), wq_ref[...]) * (hd ** -0.5)).astype(BF16)
    outs = []
    for hh in range(XA_HEADS):
        cols = slice(hh * hd, (hh + 1) * hd)
        s = _bdot_nt(q[:, cols], k_ref[:, cols])
        s = jnp.exp(s - jnp.max(s, axis=-1, keepdims=True))
        p = s / jnp.sum(s, axis=-1, keepdims=True)
        outs.append(_bdot(p, v_ref[:, cols]))
    x2 = x1 + _bdot(jnp.concatenate(outs, axis=1), wo_ref[...])
    x2_ref[...] = x2
    hf = _rms(x2, nffn_ref[...])
    hf_ref[...] = hf
    lg_ref[...] = _bdot_nt(wr_ref[...], hf) + br_ref[...]


def _attn_block(x2d, ma, mb, mc, k2d, v2d, bsz, seq, mem_len, wout, nxa, wq, wo, nffn, wr_t, br_col):
    t, d = x2d.shape
    ts = min(512, seq)
    nj = seq // ts
    row = lambda i: (i, 0)
    memrow = lambda i: (i // nj, 0)
    consts = (wout, nxa, wq, wo, nffn, wr_t, br_col)
    return pl.pallas_call(
        _attn_kernel,
        grid=(t // ts,),
        in_specs=[pl.BlockSpec((ts, d), row)] * 4 + [pl.BlockSpec((mem_len, d), memrow)] * 2
                 + [_spec(c) for c in consts],
        out_specs=[pl.BlockSpec((ts, d), row), pl.BlockSpec((ts, d), row), pl.BlockSpec((LANES, ts), lambda i: (0, i))],
        out_shape=[jax.ShapeDtypeStruct((t, d), F32), jax.ShapeDtypeStruct((t, d), F32),
                   jax.ShapeDtypeStruct((LANES, t), F32)],
        compiler_params=_cparams("arbitrary"),
        name="attn_block",
    )(x2d, ma, mb, mc, k2d, v2d, *consts)


def _first_max(rows):
    best = rows[0]
    for r in rows[1:]:
        best = jnp.maximum(best, r)
    idx = jnp.full(best.shape, len(rows) - 1, jnp.int32)
    for i in range(len(rows) - 2, -1, -1):
        idx = jnp.where(rows[i] >= best, i, idx)
    return best, idx


def _route_kernel(lg_ref, ids_ref, wts_ref, cnt_ref, carry):
    ts = lg_ref.shape[1]
    i = pl.program_id(0)

    @pl.when(i == 0)
    def _():
        carry[...] = jnp.zeros_like(carry)

    lg = lg_ref[...]
    grows = [lg[g:g + 1, :] for g in range(N_EXPERT_GROUPS)]
    gmax, grp = _first_max(grows)
    denom = grows[0] * 0.0
    for r in grows:
        denom = denom + jnp.exp(r - gmax)
    p_grp = 1.0 / denom

    erows = []
    for e in range(EXPERTS_PER_GROUP):
        r_last = N_EXPERT_GROUPS + (N_EXPERT_GROUPS - 1) * EXPERTS_PER_GROUP + e
        sel = lg[r_last:r_last + 1, :]
        for g in range(N_EXPERT_GROUPS - 2, -1, -1):
            r0 = N_EXPERT_GROUPS + g * EXPERTS_PER_GROUP + e
            sel = jnp.where(grp == g, lg[r0:r0 + 1, :], sel)
        erows.append(sel)
    v1, i1 = _first_max(erows)
    rest = [jnp.where(i1 == e, -jnp.inf, erows[e]) for e in range(EXPERTS_PER_GROUP)]
    v2, i2 = _first_max(rest)
    tt = jnp.exp(v2 - v1)
    w1 = p_grp / (1.0 + tt)
    w2 = p_grp * tt / (1.0 + tt)
    e1 = grp * EXPERTS_PER_GROUP + i1
    e2 = grp * EXPERTS_PER_GROUP + i2

    sub = lax.broadcasted_iota(jnp.int32, (N_EXPERTS, ts), 0)
    oh1 = sub == e1
    oh2 = sub == e2
    oh = jnp.where(oh1 | oh2, 1.0, 0.0)
    ti = lax.broadcasted_iota(jnp.int32, (ts, ts), 0)
    tj = lax.broadcasted_iota(jnp.int32, (ts, ts), 1)
    before = (ti < tj).astype(BF16)
    base = carry[...]
    rank = jnp.dot(oh.astype(BF16), before, preferred_element_type=F32) \
        + jnp.concatenate([base] * (ts // LANES), axis=1)
    rank1 = jnp.sum(jnp.where(oh1, rank, 0.0), axis=0, keepdims=True)
    rank2 = jnp.sum(jnp.where(oh2, rank, 0.0), axis=0, keepdims=True)
    carry[...] = base + jnp.sum(oh, axis=1, keepdims=True)

    zi = jnp.zeros((SUBLANES - 4, ts), jnp.int32)
    ids_ref[...] = jnp.concatenate([e1, e2, rank1.astype(jnp.int32), rank2.astype(jnp.int32), zi], axis=0)
    wts_ref[...] = jnp.concatenate([w1, w2, jnp.zeros((SUBLANES - 2, ts), F32)], axis=0)
    cnt_ref[...] = carry[...].astype(jnp.int32)


def _route(logits_t):
    t = logits_t.shape[1]
    ts = min(512, t)
    return pl.pallas_call(
        _route_kernel,
        grid=(t // ts,),
        in_specs=[pl.BlockSpec((LANES, ts), lambda i: (0, i))],
        out_specs=[pl.BlockSpec((SUBLANES, ts), lambda i: (0, i)), pl.BlockSpec((SUBLANES, ts), lambda i: (0, i)),
                   pl.BlockSpec((N_EXPERTS, LANES), lambda i: (0, 0))],
        out_shape=[jax.ShapeDtypeStruct((SUBLANES, t), jnp.int32), jax.ShapeDtypeStruct((SUBLANES, t), F32),
                   jax.ShapeDtypeStruct((N_EXPERTS, LANES), jnp.int32)],
        scratch_shapes=[pltpu.VMEM((N_EXPERTS, LANES), F32)],
        compiler_params=_cparams("arbitrary"),
        name="moe_route",
    )(logits_t)


def _dest_kernel(ids_ref, ps_ref, dst_ref):
    ts = ids_ref.shape[1]
    ids = ids_ref[...]
    sub = lax.broadcasted_iota(jnp.int32, (N_EXPERTS, ts), 0)
    ps = jnp.concatenate([ps_ref[...]] * (ts // LANES), axis=1)
    d1 = jnp.sum(jnp.where(sub == ids[0:1, :], ps, 0), axis=0, keepdims=True) + ids[2:3, :]
    d2 = jnp.sum(jnp.where(sub == ids[1:2, :], ps, 0), axis=0, keepdims=True) + ids[3:4, :]
    dst_ref[...] = jnp.concatenate([d1, d2, jnp.zeros((SUBLANES - 2, ts), jnp.int32)], axis=0)


def _dest(ids, pad_starts_rep):
    t = ids.shape[1]
    ts = min(2048, t)
    return pl.pallas_call(
        _dest_kernel,
        grid=(t // ts,),
        in_specs=[pl.BlockSpec((SUBLANES, ts), lambda i: (0, i)), _full(pad_starts_rep)],
        out_specs=pl.BlockSpec((SUBLANES, ts), lambda i: (0, i)),
        out_shape=jax.ShapeDtypeStruct((SUBLANES, t), jnp.int32),
        compiler_params=_cparams("arbitrary"),
        name="moe_dest",
    )(ids, pad_starts_rep)


def _hbm_row(ref, r):
    return ref.at[pl.ds(r, 1), :]


def _tile_row(ref3, g, u):
    return ref3.at[g, pl.ds(u, 1), :]


def _for_rows(n_rows, fn):
    def trip(g, c):
        for u in range(SUBLANES):
            fn(g, u)
        return c
    lax.fori_loop(0, n_rows // SUBLANES, trip, 0)


def _dispatch_kernel(d1_ref, d2_ref, pe_ref, hf_ref, xs_ref, ring, zbuf, in_sem, out_sem, z_sem, *, n_tiles):
    tg = ring.shape[1]
    ts = tg * SUBLANES
    n_blocks = xs_ref.shape[0] // MOE_BLOCK
    i = pl.program_id(0)
    slot = i % DISPATCH_SLOTS

    def load(tile, s):
        return pltpu.make_async_copy(hf_ref.at[pl.ds(tile * tg, tg)], ring.at[s], in_sem.at[s])

    def drain(s):
        for _ in range(2):
            pltpu.make_async_copy(hf_ref.at[pl.ds(0, tg)], ring.at[s], out_sem.at[s]).wait()

    def zero_block(first_row):
        rows = pl.ds(pl.multiple_of(first_row, MOE_BLOCK), MOE_BLOCK)
        return pltpu.make_async_copy(zbuf, xs_ref.at[rows, :], z_sem)

    @pl.when(i == 0)
    def _():
        load(0, 0).start()
        zbuf[...] = jnp.zeros_like(zbuf)
        for start in (True, False):
            for e in range(N_EXPERTS):
                prev_end = pe_ref[e - 1] if e else 0

                @pl.when(pe_ref[e] > prev_end)
                def _():
                    cp = zero_block(pe_ref[e] - MOE_BLOCK)
                    cp.start() if start else cp.wait()
            for blk in range(n_blocks - N_EXPERTS, n_blocks):
                @pl.when(blk * MOE_BLOCK >= pe_ref[N_EXPERTS - 1])
                def _():
                    cp = zero_block(blk * MOE_BLOCK)
                    cp.start() if start else cp.wait()

    nxt = (i + 1) % DISPATCH_SLOTS

    @pl.when(i + 1 < n_tiles)
    def _():
        @pl.when(i + 1 >= DISPATCH_SLOTS)
        def _():
            drain(nxt)
        load(i + 1, nxt).start()

    load(i, slot).wait()
    base = i * ts
    src = ring.at[slot]
    sem = out_sem.at[slot]

    def scatter(g, u):
        tok = base + g * SUBLANES + u
        pltpu.make_async_copy(_tile_row(src, g, u), _hbm_row(xs_ref, d1_ref[tok]), sem).start()
        pltpu.make_async_copy(_tile_row(src, g, u), _hbm_row(xs_ref, d2_ref[tok]), sem).start()

    _for_rows(ts, scatter)

    @pl.when(i == n_tiles - 1)
    def _():
        for back in range(min(DISPATCH_SLOTS, n_tiles)):
            drain((n_tiles - 1 - back) % DISPATCH_SLOTS)


def _dispatch(d1, d2, pad_ends, hf, n_slots):
    t, d = hf.shape
    ts = min(256, t)
    n_tiles = t // ts
    hf = hf.reshape(t // SUBLANES, SUBLANES, d)
    grid_spec = pltpu.PrefetchScalarGridSpec(
        num_scalar_prefetch=3,
        grid=(n_tiles,),
        in_specs=[pl.BlockSpec(memory_space=pl.ANY)],
        out_specs=pl.BlockSpec(memory_space=pl.ANY),
        scratch_shapes=[pltpu.VMEM((DISPATCH_SLOTS, ts // SUBLANES, SUBLANES, d), F32), pltpu.VMEM((MOE_BLOCK, d), F32),
                        pltpu.SemaphoreType.DMA((DISPATCH_SLOTS,)), pltpu.SemaphoreType.DMA((DISPATCH_SLOTS,)),
                        pltpu.SemaphoreType.DMA],
    )
    return pl.pallas_call(
        functools.partial(_dispatch_kernel, n_tiles=n_tiles),
        grid_spec=grid_spec,
        out_shape=jax.ShapeDtypeStruct((n_slots, d), F32),
        compiler_params=_cparams("arbitrary"),
        name="moe_dispatch",
    )(d1, d2, pad_ends, hf)


def _expert_kernel(be_ref, nu_ref, xs_ref, wgu_ref, wdn_ref, y_ref, wgu_b, wdn_b):
    b = pl.program_id(0)
    de = wdn_ref.shape[0]

    @pl.when(b < nu_ref[0])
    def _():
        @pl.when((b == 0) | (be_ref[b] != be_ref[jnp.maximum(b - 1, 0)]))
        def _():
            wgu_b[...] = wgu_ref[...].astype(BF16)
            wdn_b[...] = wdn_ref[...].astype(BF16)

        gu = _bdot(xs_ref[...], wgu_b[...])
        act = _silu(gu[:, :de]) * gu[:, de:]
        y_ref[...] = _bdot(act, wdn_b[...])

    @pl.when(b >= nu_ref[0])
    def _():
        y_ref[...] = jnp.zeros_like(y_ref)


def _experts(block_e, n_used, xs, wgu, wdn, layer):
    n_slots, d = xs.shape
    n_blocks = n_slots // MOE_BLOCK

    def blk(b, be, nu):
        return jnp.minimum(b, nu[0] - 1)

    def slab(w):
        return pl.BlockSpec((None, None) + w.shape[2:], lambda b, be, nu: (layer, be[blk(b, be, nu)], 0, 0))

    grid_spec = pltpu.PrefetchScalarGridSpec(
        num_scalar_prefetch=2,
        grid=(n_blocks,),
        in_specs=[pl.BlockSpec((MOE_BLOCK, d), lambda b, be, nu: (blk(b, be, nu), 0)), slab(wgu), slab(wdn)],
        out_specs=pl.BlockSpec((MOE_BLOCK, d), lambda b, be, nu: (b, 0)),
        scratch_shapes=[pltpu.VMEM(wgu.shape[2:], BF16), pltpu.VMEM(wdn.shape[2:], BF16)],
    )
    return pl.pallas_call(
        _expert_kernel,
        grid_spec=grid_spec,
        out_shape=jax.ShapeDtypeStruct((n_slots, d), F32),
        compiler_params=_cparams("arbitrary"),
        name="moe_experts",
    )(block_e, n_used, xs, wgu, wdn)


def _combine_kernel(d1_ref, d2_ref, x_ref, w1_ref, w2_ref, nf_ref, yb_ref, o_ref, buf1, buf2, sem, *, n_tiles):
    ts = x_ref.shape[0]
    i = pl.program_id(0)
    slot = i % 2

    def gather(tile, s):
        base = tile * ts
        b1, b2, sm = buf1.at[s], buf2.at[s], sem.at[s]

        def one(g, u):
            tok = base + g * SUBLANES + u
            pltpu.make_async_copy(_hbm_row(yb_ref, d1_ref[tok]), _tile_row(b1, g, u), sm).start()
            pltpu.make_async_copy(_hbm_row(yb_ref, d2_ref[tok]), _tile_row(b2, g, u), sm).start()

        _for_rows(ts, one)

    @pl.when(i == 0)
    def _():
        gather(0, 0)

    @pl.when(i + 1 < n_tiles)
    def _():
        gather(i + 1, 1 - slot)

    pltpu.make_async_copy(buf2.at[1 - slot], buf1.at[slot], sem.at[slot]).wait()
    pltpu.make_async_copy(buf1.at[1 - slot], buf2.at[slot], sem.at[slot]).wait()
    d = x_ref.shape[1]
    out = x_ref[...] + w1_ref[...] * buf1[slot].reshape(ts, d) + w2_ref[...] * buf2[slot].reshape(ts, d)
    o_ref[...] = _rms(out, nf_ref[...])


def _combine_final(d1, d2, x2, w1c, w2c, nf, yb):
    t, d = x2.shape
    ts = min(256, t)
    row = lambda i, a, b: (i, 0)
    grid_spec = pltpu.PrefetchScalarGridSpec(
        num_scalar_prefetch=2,
        grid=(t // ts,),
        in_specs=[pl.BlockSpec((ts, d), row), pl.BlockSpec((ts, 1), row), pl.BlockSpec((ts, 1), row),
                  pl.BlockSpec((1, d), lambda i, a, b: (0, 0)), pl.BlockSpec(memory_space=pl.ANY)],
        out_specs=pl.BlockSpec((ts, d), row),
        scratch_shapes=[pltpu.VMEM((2, ts // SUBLANES, SUBLANES, d), F32),
                        pltpu.VMEM((2, ts // SUBLANES, SUBLANES, d), F32), pltpu.SemaphoreType.DMA((2,))],
    )
    return pl.pallas_call(
        functools.partial(_combine_kernel, n_tiles=t // ts),
        grid_spec=grid_spec,
        out_shape=jax.ShapeDtypeStruct((t, d), F32),
        compiler_params=_cparams("arbitrary"),
        name="moe_combine",
    )(d1, d2, x2, w1c, w2c, nf, yb)


def _moe_experts(hf, logits_t, wgu, wdn, layer):
    t, d = hf.shape
    ids, wts, counts = _route(logits_t)
    counts = counts[:, 0]
    padded = (counts + MOE_BLOCK - 1) // MOE_BLOCK * MOE_BLOCK
    pad_ends = jnp.cumsum(padded)
    pad_starts = pad_ends - padded
    n_blocks = (2 * t + MOE_BLOCK - 1) // MOE_BLOCK + N_EXPERTS
    block_start = jnp.arange(n_blocks, dtype=jnp.int32) * MOE_BLOCK
    block_e = jnp.minimum(jnp.sum(pad_ends[None, :] <= block_start[:, None], axis=1), N_EXPERTS - 1).astype(jnp.int32)
    n_used = (pad_ends[-1:] // MOE_BLOCK).astype(jnp.int32)
    dst = _dest(ids, jnp.broadcast_to(pad_starts[:, None].astype(jnp.int32), (N_EXPERTS, LANES)))
    d1, d2 = dst[0], dst[1]
    xs = _dispatch(d1, d2, pad_ends.astype(jnp.int32), hf, n_blocks * MOE_BLOCK)
    yb = _experts(block_e, n_used, xs, wgu, wdn, layer)
    return d1, d2, wts[0].reshape(t, 1), wts[1].reshape(t, 1), yb


def _pair_blockdiag(w):
    z = jnp.zeros_like(w[0::2])
    top = jnp.concatenate([w[0::2], z], axis=2)
    bot = jnp.concatenate([z, w[1::2]], axis=2)
    return jnp.concatenate([top, bot], axis=1)


def kernel(x, mem, norm_mix, w_in, conv_a_w, conv_a_b, lru_gx_w, lru_gx_b, lru_ga_w, lru_ga_b, lru_lambda, sgu_norm, sgu_w, sgu_b, conv_c_w, conv_c_b, ssm_dt_bias, ssm_a_log, ssm_d, ssm_norm, w_a_proj, w_b_proj, w_c_proj, w_out, norm_xa, norm_mem, w_q, w_kv, w_o, norm_ffn, w_router_group, b_router_group, w_router_expert, b_router_expert, w_gate_up, w_down, norm_final):
    bsz, seq, d = x.shape
    mem_len = mem.shape[1]
    depth = w_in.shape[0]
    lru_w = conv_a_w.shape[2]
    sgu_wd = sgu_norm.shape[1]
    ssm_w = ssm_norm.shape[1]
    conv_dim = conv_c_w.shape[2]
    off_ya = lru_w
    off_uv = 2 * lru_w
    off_z = off_uv + 2 * sgu_wd
    off_xbc = off_z + ssm_w
    off_dt = off_xbc + conv_dim
    off_gate = off_dt + SSM_HEADS
    t = bsz * seq
    row = lambda v: v.reshape(1, -1).astype(F32)

    expand = (jnp.arange(LANES)[:, None] == (jnp.arange(ssm_w)[None, :] // SSM_HEAD_DIM)).astype(BF16)

    xcur = x.reshape(t, d)
    mem2d = mem.reshape(bsz * mem_len, d)
    pending = None
    for l in range(depth):
        win = w_in[l].astype(BF16)
        nrm = row(norm_mix[l])
        gates = win[:, off_gate:]
        gw = jnp.concatenate([_pair_blockdiag(lru_gx_w[l]), _pair_blockdiag(lru_ga_w[l])], axis=2).astype(BF16)
        sgu_bias = jnp.repeat(sgu_b[l].T, sgu_wd // SGU_GROUPS, axis=1)
        m_b = _sgu_branch(xcur, seq, nrm, _Cols(win, off_uv, sgu_wd), _Cols(win, off_uv + sgu_wd, sgu_wd),
                          _Cols(gates, d, d), row(sgu_norm[l]), sgu_w[l], sgu_bias, w_b_proj[l].astype(BF16),
                          pending=pending)
        if pending is not None:
            m_b, xcur = m_b
        m_a = _lru_branch(xcur, bsz, seq, nrm, _Cols(win, 0, lru_w), _Cols(win, off_ya, lru_w), _Cols(gates, 0, d),
                          conv_a_w[l], row(conv_a_b[l]), gw, row(lru_gx_b[l]), row(lru_ga_b[l]),
                          row(lru_lambda[l]), w_a_proj[l].astype(BF16))
        pad16 = lambda v: jnp.pad(v.reshape(1, -1), ((0, 0), (0, LANES - SSM_HEADS)))
        m_c = _ssd_branch(xcur, bsz, seq, nrm, _Cols(win, off_z, ssm_w), _Cols(win, off_xbc, ssm_w),
                          _Cols(win, off_xbc + ssm_w, conv_dim - ssm_w), _Cols(win, off_dt, LANES),
                          _Cols(gates, 2 * d, d), conv_c_w[l], row(conv_c_b[l]), pad16(ssm_dt_bias[l]),
                          pad16(ssm_a_log[l]), row(jnp.repeat(ssm_d[l], SSM_HEAD_DIM)), expand,
                          row(ssm_norm[l]), w_c_proj[l].astype(BF16))
        k2d, v2d = _kv_proj(mem2d, row(norm_mem[l]), w_kv[l].astype(BF16))
        n_r = N_EXPERT_GROUPS + N_EXPERTS
        wr_t = jnp.pad(jnp.concatenate([w_router_group[l], w_router_expert[l]], axis=1).T,
                       ((0, LANES - n_r), (0, 0))).astype(BF16)
        br_col = jnp.pad(jnp.concatenate([b_router_group[l], b_router_expert[l]]), (0, LANES - n_r)).reshape(LANES, 1)
        x2, hf, logits_t = _attn_block(xcur, m_a, m_b, m_c, k2d, v2d, bsz, seq, mem_len, w_out[l].astype(BF16),
                                       row(norm_xa[l]), w_q[l].astype(BF16), w_o[l].astype(BF16),
                                       row(norm_ffn[l]), wr_t, br_col)
        xcur = x2
        pending = _moe_experts(hf, logits_t, w_gate_up, w_down, l)
    d1, d2, w1c, w2c, yb = pending
    return _combine_final(d1, d2, xcur, w1c, w2c, row(norm_final), yb).reshape(bsz, seq, d)
```

```python
import functools
from typing import NamedTuple

import jax
import jax.numpy as jnp
from jax import lax
from jax.experimental import pallas as pl
from jax.experimental.pallas import tpu as pltpu

F32 = jnp.float32
BF16 = jnp.bfloat16

EPS = 1e-6
TINY = float(jnp.finfo(jnp.float32).tiny)
CONV_WIDTH = 4
CONV_PAD = 8
LRU_HEADS = 8
LRU_C = 8.0
SGU_GROUPS = 4
SGU_BLOCK = 128
SGU_CHUNK = 64
SSM_HEADS = 16
SSM_HEAD_DIM = 64
SSM_GROUPS = 4
SSM_STATE = 128
SSD_CHUNK = 128
XA_HEADS = 4
N_EXPERT_GROUPS = 4
EXPERTS_PER_GROUP = 8
N_EXPERTS = N_EXPERT_GROUPS * EXPERTS_PER_GROUP
MOE_BLOCK = 512
DISPATCH_SLOTS = 3
COMBINE_ROW_BUFFERS = 3
LANES = 128
SUBLANES = 8
VMEM_LIMIT_BYTES = 56 * 1024 * 1024


def _cparams(*sem):
    return pltpu.CompilerParams(dimension_semantics=sem, vmem_limit_bytes=VMEM_LIMIT_BYTES)


def _full(a):
    nd = a.ndim
    return pl.BlockSpec(a.shape, lambda *_: (0,) * nd)


class _Cols(NamedTuple):
    arr: jax.Array
    off: int
    width: int


def _spec(c):
    if isinstance(c, _Cols):
        blk, rem = divmod(c.off, c.width)
        assert rem == 0
        return pl.BlockSpec((c.arr.shape[0], c.width), lambda *_: (0, blk))
    return _full(c)


def _arr(c):
    return c.arr if isinstance(c, _Cols) else c


def _bdot(a, b):
    return jnp.dot(a.astype(BF16), b.astype(BF16), preferred_element_type=F32)


def _bdot_nt(a, b):
    return lax.dot_general(a.astype(BF16), b.astype(BF16), (((1,), (1,)), ((), ())),
                           preferred_element_type=F32)


def _split3(v):
    hi = v.astype(BF16)
    r1 = v - hi.astype(F32)
    mid = r1.astype(BF16)
    lo = (r1 - mid.astype(F32)).astype(BF16)
    return hi, mid, lo


def _dot_exact_rhs(v, m01):
    hi, mid, lo = _split3(v)
    d = functools.partial(jnp.dot, preferred_element_type=F32)
    return d(hi, m01) + d(mid, m01) + d(lo, m01)


def _dot_exact_lhs(m01, v):
    hi, mid, lo = _split3(v)
    d = functools.partial(jnp.dot, preferred_element_type=F32)
    return d(m01, hi) + d(m01, mid) + d(m01, lo)


def _rms(x, g):
    return x * lax.rsqrt(jnp.mean(x * x, axis=-1, keepdims=True) + EPS) * g


def _sigmoid(x):
    return 1.0 / (1.0 + jnp.exp(-x))


def _silu(x):
    return x * _sigmoid(x)


def _softplus(x):
    return jnp.maximum(x, 0.0) + jnp.log1p(jnp.exp(-jnp.abs(x)))


def _gelu(x):
    return jax.nn.gelu(x)


def _conv4(buf, tile_rows, xin, cw, cb):
    buf[pl.ds(CONV_PAD, tile_rows), :] = xin
    acc = cb
    for k in range(CONV_WIDTH):
        acc = acc + cw[k:k + 1, :] * buf[pl.ds(CONV_PAD - (CONV_WIDTH - 1) + k, tile_rows), :]
    buf[pl.ds(0, CONV_PAD), :] = buf[pl.ds(tile_rows, CONV_PAD), :]
    return acc


def _to_phases(slab, v):
    n_slab, rows, _ = slab.shape
    for q in range(n_slab):
        slab[q] = v[:, q * LANES:(q + 1) * LANES]
    return [jnp.concatenate([slab[q, pl.ds(s, rows // SUBLANES, stride=SUBLANES), :] for q in range(n_slab)], axis=1)
            for s in range(SUBLANES)]


def _from_phases(slab, phases):
    n_slab, rows, _ = slab.shape
    for s, v in enumerate(phases):
        for q in range(n_slab):
            slab[q, pl.ds(s, rows // SUBLANES, stride=SUBLANES), :] = v[:, q * LANES:(q + 1) * LANES]
    return jnp.concatenate([slab[q] for q in range(n_slab)], axis=1)


def _phase_conv(xp, hist, cw, cb):
    ng, width = xp[0].shape
    taps = CONV_WIDTH - 1
    first_group = lax.broadcasted_iota(jnp.int32, (ng, width), 0) == 0
    back = [jnp.where(first_group, hist[i:i + 1, :], pltpu.roll(xp[SUBLANES - taps + i], 1, 0)) for i in range(taps)]
    src = back + xp
    out = []
    for s in range(SUBLANES):
        acc = cb
        for k in range(CONV_WIDTH):
            acc = acc + cw[k:k + 1, :] * src[s + k]
        out.append(acc)
    for i in range(taps):
        hist[i:i + 1, :] = xp[SUBLANES - taps + i][ng - 1:ng, :]
    return out


def _lru_kernel(x_ref, nrm_ref, wxa_ref, wya_ref, wg_ref, cw_ref, cb_ref, gw_ref, gxb_ref, gab_ref,
                lam_ref, wp_ref, o_ref, slab, p_buf, h_buf, c_buf, hist):
    n_slab, ts, _ = slab.shape
    width = n_slab * LANES
    ng = ts // SUBLANES
    taps = CONV_WIDTH - 1
    j = pl.program_id(1)

    @pl.when(j == 0)
    def _():
        hist[...] = jnp.zeros_like(hist)

    h = _rms(x_ref[...], nrm_ref[...]).astype(BF16)
    xp = _to_phases(slab, jnp.dot(h, wxa_ref[...], preferred_element_type=F32))
    xc = jnp.concatenate(_phase_conv(xp, hist, cw_ref[...], cb_ref[...]), axis=0)

    pair = 2 * (width // LRU_HEADS)
    gxs, gas = [], []
    for p in range(LRU_HEADS // 2):
        g = _bdot(xc[:, p * pair:(p + 1) * pair], gw_ref[p])
        gxs.append(g[:, :pair])
        gas.append(g[:, pair:])
    gate_x = _sigmoid(jnp.concatenate(gxs, axis=1) + gxb_ref[...])
    gate_a = _sigmoid(jnp.concatenate(gas, axis=1) + gab_ref[...])
    log_a = (-LRU_C * gate_a) * _softplus(-lam_ref[...])
    a = jnp.exp(log_a)
    m = jnp.maximum(1.0 - a * a, 0.0)
    mult = m * lax.rsqrt(jnp.maximum(m, TINY))
    b = xc * gate_x * mult

    hs = [b[:ng]]
    ps = [a[:ng]]
    for s in range(1, SUBLANES):
        a_s = a[s * ng:(s + 1) * ng]
        hs.append(a_s * hs[-1] + b[s * ng:(s + 1) * ng])
        ps.append(a_s * ps[-1])
    p_buf[...] = ps[-1]
    h_buf[...] = hs[-1]
    state = jnp.broadcast_to(hist[taps:taps + 1, :], (SUBLANES, width))
    for g in range(ng):
        c_buf[g:g + 1, :] = state[0:1, :]
        state = (jnp.broadcast_to(p_buf[g:g + 1, :], (SUBLANES, width)) * state
                 + jnp.broadcast_to(h_buf[g:g + 1, :], (SUBLANES, width)))
    hist[taps:taps + 1, :] = state[0:1, :]
    enter = c_buf[...]
    hn = _from_phases(slab, [hs[s] + ps[s] * enter for s in range(SUBLANES)])

    ya = jnp.dot(h, wya_ref[...], preferred_element_type=F32)
    out = hn * _gelu(ya)
    y = _bdot(out, wp_ref[...])
    gate = _sigmoid(jnp.dot(h, wg_ref[...], preferred_element_type=F32))
    o_ref[...] = (gate * y).astype(o_ref.dtype)


def _lru_branch(x2d, bsz, seq, nrm, wxa, wya, wg, cw, cb, gw, gxb, gab, lam, wp):
    t, d = x2d.shape
    width = cw.shape[1]
    ts = min(512, seq)
    nj = seq // ts
    row = lambda b, j: (b * nj + j, 0)
    consts = (nrm, wxa, wya, wg, cw, cb, gw, gxb, gab, lam, wp)
    return pl.pallas_call(
        _lru_kernel,
        grid=(bsz, nj),
        in_specs=[pl.BlockSpec((ts, d), row)] + [_spec(c) for c in consts],
        out_specs=pl.BlockSpec((ts, d), row),
        out_shape=jax.ShapeDtypeStruct((t, d), BF16),
        scratch_shapes=[pltpu.VMEM((width // LANES, ts, LANES), F32)]
                       + [pltpu.VMEM((ts // SUBLANES, width), F32)] * 3 + [pltpu.VMEM((SUBLANES, width), F32)],
        compiler_params=_cparams("arbitrary", "arbitrary"),
        name="lru_branch",
    )(x2d, *[_arr(c) for c in consts])


def _sgu_kernel(x_ref, *rest):
    _sgu_body(x_ref[...], *rest)


def _sgu_combine_kernel(d1_ref, d2_ref, x_ref, w1_ref, w2_ref, yb_ref, *rest, n_tiles):
    *sgu_refs, x3_ref, mix_buf, buf1, buf2, sem = rest
    ts, d = x_ref.shape
    i = pl.program_id(0)
    n_buf = buf1.shape[0]
    ahead = n_buf - 1
    slot = i % n_buf

    def row_copies(tile, s, g, u):
        tok = tile * ts + g * SUBLANES + u
        pltpu.make_async_copy(_hbm_row(yb_ref, d1_ref[tok]), _tile_row(buf1.at[s], g, u), sem.at[s]).start()
        pltpu.make_async_copy(_hbm_row(yb_ref, d2_ref[tok]), _tile_row(buf2.at[s], g, u), sem.at[s]).start()

    def wait_rows(s):
        other = (s + 1) % n_buf
        pltpu.make_async_copy(buf2.at[other], buf1.at[s], sem.at[s]).wait()
        pltpu.make_async_copy(buf1.at[other], buf2.at[s], sem.at[s]).wait()

    def last(tile):
        return jnp.minimum(tile, n_tiles - 1)

    @pl.when(i == 0)
    def _():
        for k in range(ahead):
            _for_rows(ts, functools.partial(row_copies, last(k), k))

    wait_rows(slot)
    x = x_ref[...] + w1_ref[...] * buf1[slot].reshape(ts, d) + w2_ref[...] * buf2[slot].reshape(ts, d)
    x3_ref[...] = x
    nxt, nxt_slot = last(i + ahead), (i + ahead) % n_buf
    for r in range(ts):
        row_copies(nxt, nxt_slot, r // SUBLANES, r % SUBLANES)
    _sgu_body(x, *sgu_refs, mix_buf)

    @pl.when(i == n_tiles - 1)
    def _():
        for k in range(1, n_buf):
            wait_rows((i + k) % n_buf)


def _sgu_body(x, nrm_ref, wu_ref, wv_ref, wg_ref, ng_ref, ws_ref, bias_ref, wp_ref, o_ref, mix_buf):
    ts, width = mix_buf.shape
    gdim = width // SGU_GROUPS
    h = _rms(x, nrm_ref[...]).astype(BF16)
    u = _gelu(jnp.dot(h, wu_ref[...], preferred_element_type=F32))
    v = _gelu(jnp.dot(h, wv_ref[...], preferred_element_type=F32))
    vc = v - jnp.mean(v, axis=-1, keepdims=True)
    v = (vc * lax.rsqrt(jnp.mean(vc * vc, axis=-1, keepdims=True) + EPS) * ng_ref[...]).astype(BF16)

    ti = lax.broadcasted_iota(jnp.int32, (SGU_BLOCK, SGU_BLOCK), 0) // SGU_CHUNK
    si = lax.broadcasted_iota(jnp.int32, (SGU_BLOCK, SGU_BLOCK), 1) // SGU_CHUNK
    mask = si <= ti
    wm = [jnp.where(mask, ws_ref[g], 0.0).astype(BF16) for g in range(SGU_GROUPS)]
    for blk in range(ts // SGU_BLOCK):
        rows = slice(blk * SGU_BLOCK, (blk + 1) * SGU_BLOCK)
        for g in range(SGU_GROUPS):
            cols = slice(g * gdim, (g + 1) * gdim)
            mix_buf[rows, cols] = jnp.dot(wm[g], v[rows, cols], preferred_element_type=F32) + bias_ref[:, cols]
    out = u * mix_buf[...]
    y = _bdot(out, wp_ref[...])
    gate = _sigmoid(jnp.dot(h, wg_ref[...], preferred_element_type=F32))
    o_ref[...] = (gate * y).astype(o_ref.dtype)


def _sgu_branch(x2d, seq, nrm, wu, wv, wg, ng, ws, bias, wp, pending=None):
    t, d = x2d.shape
    width = ng.shape[1]
    ts = min(512, seq)
    consts = (nrm, wu, wv, wg, ng, ws, bias, wp)
    if pending is not None:
        d1, d2, w1c, w2c, yb = pending
        row = lambda i, a, b: (i, 0)
        grid_spec = pltpu.PrefetchScalarGridSpec(
            num_scalar_prefetch=2,
            grid=(t // ts,),
            in_specs=[pl.BlockSpec((ts, d), row), pl.BlockSpec((ts, 1), row), pl.BlockSpec((ts, 1), row),
                      pl.BlockSpec(memory_space=pl.ANY)] + [_spec(c) for c in consts],
            out_specs=[pl.BlockSpec((ts, d), row), pl.BlockSpec((ts, d), row)],
            scratch_shapes=[pltpu.VMEM((ts, width), F32),
                            pltpu.VMEM((COMBINE_ROW_BUFFERS, ts // SUBLANES, SUBLANES, d), F32),
                            pltpu.VMEM((COMBINE_ROW_BUFFERS, ts // SUBLANES, SUBLANES, d), F32),
                            pltpu.SemaphoreType.DMA((COMBINE_ROW_BUFFERS,))],
        )
        return pl.pallas_call(
            functools.partial(_sgu_combine_kernel, n_tiles=t // ts),
            grid_spec=grid_spec,
            out_shape=[jax.ShapeDtypeStruct((t, d), BF16), jax.ShapeDtypeStruct((t, d), F32)],
            compiler_params=_cparams("arbitrary"),
            name="sgu_branch_combine",
        )(d1, d2, x2d, w1c, w2c, yb, *[_arr(c) for c in consts])
    return pl.pallas_call(
        _sgu_kernel,
        grid=(t // ts,),
        in_specs=[pl.BlockSpec((ts, d), lambda i: (i, 0))] + [_spec(c) for c in consts],
        out_specs=pl.BlockSpec((ts, d), lambda i: (i, 0)),
        out_shape=jax.ShapeDtypeStruct((t, d), BF16),
        scratch_shapes=[pltpu.VMEM((ts, width), F32)],
        compiler_params=_cparams("arbitrary"),
        name="sgu_branch",
    )(x2d, *[_arr(c) for c in consts])


def _ssd_kernel(x_ref, nrm_ref, wz_ref, wx_ref, wbc_ref, wdt_ref, wg_ref, cw_ref, cb_ref, dtb_ref, alog_ref,
                dskip_ref, expand_ref, ng_ref, wp_ref, o_ref, xbc_buf, y_buf, state):
    ts, width = y_buf.shape
    lc = SSD_CHUNK
    n_state = SSM_STATE
    gw = width // SSM_GROUPS
    hpg = SSM_HEADS // SSM_GROUPS
    p_dim = SSM_HEAD_DIM
    j = pl.program_id(1)

    @pl.when(j == 0)
    def _():
        xbc_buf[pl.ds(0, CONV_PAD), :] = jnp.zeros((CONV_PAD, xbc_buf.shape[1]), F32)
        state[...] = jnp.zeros_like(state)

    h = _rms(x_ref[...], nrm_ref[...]).astype(BF16)
    xbc = jnp.concatenate([jnp.dot(h, wx_ref[...], preferred_element_type=F32),
                           jnp.dot(h, wbc_ref[...], preferred_element_type=F32)], axis=1)
    xc = _silu(_conv4(xbc_buf, ts, xbc, cw_ref[...], cb_ref[...]))
    lane = lax.broadcasted_iota(jnp.int32, (1, LANES), 1)
    heads = lane < SSM_HEADS
    dt = jnp.where(heads, _softplus(jnp.dot(h, wdt_ref[...], preferred_element_type=F32) + dtb_ref[...]), 0.0)
    a_row = jnp.where(heads, -jnp.exp(alog_ref[...]), 0.0)
    da = dt * a_row

    li = lax.broadcasted_iota(jnp.int32, (lc, lc), 0)
    si = lax.broadcasted_iota(jnp.int32, (lc, lc), 1)
    causal = li >= si
    tri = causal.astype(BF16)
    plane = lax.broadcasted_iota(jnp.int32, (lc, 2 * p_dim), 1)
    expand = expand_ref[...]

    for c in range(ts // lc):
        rows = slice(c * lc, (c + 1) * lc)
        dt_c = dt[rows]
        cs = _dot_exact_lhs(tri, da[rows])
        cs_t = cs.T
        dt_t = dt_c.T
        cs_last = cs[lc - 1:lc, :]
        wst_e = _bdot(jnp.exp(cs_last - cs) * dt_c, expand)
        ecs_e = _bdot(jnp.exp(cs), expand)
        cd_e = _dot_exact_rhs(jnp.broadcast_to(jnp.exp(cs_last), (SUBLANES, LANES)), expand)[0:1, :]
        xs_c = xc[rows, :width]
        xw = (xs_c * wst_e).astype(BF16)
        xs_b = xs_c.astype(BF16)
        for g in range(SSM_GROUPS):
            bm = xc[rows, width + g * n_state: width + (g + 1) * n_state]
            cm = xc[rows, width + (SSM_GROUPS + g) * n_state: width + (SSM_GROUPS + g + 1) * n_state].astype(BF16)
            cb = _bdot_nt(cm, bm)
            st = state[g]
            y_g = jnp.dot(cm, st.astype(BF16), preferred_element_type=F32) * ecs_e[:, g * gw:(g + 1) * gw]
            yds = []
            for pr in range(hpg // 2):
                ws = []
                for r in (2 * pr, 2 * pr + 1):
                    hh = g * hpg + r
                    seg = cs[:, hh:hh + 1] - cs_t[hh:hh + 1, :]
                    ws.append((cb * jnp.exp(jnp.where(causal, seg, -jnp.inf)) * dt_t[hh:hh + 1, :]).astype(BF16))
                xp = xs_b[:, g * gw + pr * 2 * p_dim: g * gw + (pr + 1) * 2 * p_dim]
                zero = jnp.zeros_like(xp)
                rhs = jnp.concatenate([jnp.where(plane < p_dim, xp, zero), jnp.where(plane >= p_dim, xp, zero)], axis=0)
                yds.append(jnp.dot(jnp.concatenate(ws, axis=1), rhs, preferred_element_type=F32))
            y_buf[rows, g * gw:(g + 1) * gw] = y_g + jnp.concatenate(yds, axis=1)
            new = jnp.dot(bm.T.astype(BF16), xw[:, g * gw:(g + 1) * gw], preferred_element_type=F32)
            state[g] = st * cd_e[:, g * gw:(g + 1) * gw] + new

    z = jnp.dot(h, wz_ref[...], preferred_element_type=F32)
    y = y_buf[...] + xc[:, :width] * dskip_ref[...]
    yg = y * _silu(z)
    ng = ng_ref[...]
    parts = []
    for g in range(SSM_GROUPS):
        seg = yg[:, g * gw:(g + 1) * gw]
        parts.append(seg * lax.rsqrt(jnp.mean(seg * seg, axis=-1, keepdims=True) + EPS) * ng[:, g * gw:(g + 1) * gw])
    yn = jnp.concatenate(parts, axis=1)
    out = _bdot(yn, wp_ref[...])
    gate = _sigmoid(jnp.dot(h, wg_ref[...], preferred_element_type=F32))
    o_ref[...] = (gate * out).astype(o_ref.dtype)


def _ssd_branch(x2d, bsz, seq, nrm, wz, wx, wbc, wdt, wg, cw, cb, dtb, alog, dskip, expand, ng, wp):
    t, d = x2d.shape
    width = ng.shape[1]
    ts = min(512, seq)
    nj = seq // ts
    row = lambda b, j: (b * nj + j, 0)
    consts = (nrm, wz, wx, wbc, wdt, wg, cw, cb, dtb, alog, dskip, expand, ng, wp)
    return pl.pallas_call(
        _ssd_kernel,
        grid=(bsz, nj),
        in_specs=[pl.BlockSpec((ts, d), row)] + [_spec(c) for c in consts],
        out_specs=pl.BlockSpec((ts, d), row),
        out_shape=jax.ShapeDtypeStruct((t, d), BF16),
        scratch_shapes=[pltpu.VMEM((ts + CONV_PAD, cw.shape[1]), F32), pltpu.VMEM((ts, width), F32),
                        pltpu.VMEM((SSM_GROUPS, SSM_STATE, width // SSM_GROUPS), F32)],
        compiler_params=_cparams("arbitrary", "arbitrary"),
        name="ssd_branch",
    )(x2d, *[_arr(c) for c in consts])


def _kv_kernel(m_ref, nrm_ref, w_ref, k_ref, v_ref):
    d = k_ref.shape[1]
    h = _rms(m_ref[...], nrm_ref[...]).astype(BF16)
    kv = jnp.dot(h, w_ref[...], preferred_element_type=F32)
    k_ref[...] = kv[:, :d].astype(k_ref.dtype)
    v_ref[...] = kv[:, d:].astype(v_ref.dtype)


def _kv_proj(mem2d, nrm, wkv):
    t, d = mem2d.shape
    ts = min(512, t)
    return pl.pallas_call(
        _kv_kernel,
        grid=(t // ts,),
        in_specs=[pl.BlockSpec((ts, d), lambda i: (i, 0)), _full(nrm), _full(wkv)],
        out_specs=[pl.BlockSpec((ts, d), lambda i: (i, 0))] * 2,
        out_shape=[jax.ShapeDtypeStruct((t, d), BF16)] * 2,
        compiler_params=_cparams("arbitrary"),
        name="kv_proj",
    )(mem2d, nrm, wkv)


def _attn_kernel(x_ref, ma_ref, mb_ref, mc_ref, k_ref, v_ref, wout_ref, nxa_ref, wq_ref, wo_ref, nffn_ref,
                 wr_ref, br_ref, x2_ref, hf_ref, lg_ref):
    d = x_ref.shape[1]
    hd = d // XA_HEADS
    merged = ma_ref[...].astype(F32) + mb_ref[...].astype(F32) + mc_ref[...].astype(F32)
    x1 = x_ref[...] + _bdot(merged, wout_ref[...])
    q = (_bdot(_rms(x1, nxa_ref[...]), wq_ref[...]) * (hd ** -0.5)).astype(BF16)
    outs = []
    for hh in range(XA_HEADS):
        cols = slice(hh * hd, (hh + 1) * hd)
        s = _bdot_nt(q[:, cols], k_ref[:, cols])
        s = jnp.exp(s - jnp.max(s, axis=-1, keepdims=True))
        p = s / jnp.sum(s, axis=-1, keepdims=True)
        outs.append(_bdot(p, v_ref[:, cols]))
    x2 = x1 + _bdot(jnp.concatenate(outs, axis=1), wo_ref[...])
    x2_ref[...] = x2
    hf = _rms(x2, nffn_ref[...])
    hf_ref[...] = hf
    lg_ref[...] = _bdot_nt(wr_ref[...], hf) + br_ref[...]


def _attn_block(x2d, ma, mb, mc, k2d, v2d, bsz, seq, mem_len, wout, nxa, wq, wo, nffn, wr_t, br_col):
    t, d = x2d.shape
    ts = min(512, seq)
    nj = seq // ts
    row = lambda i: (i, 0)
    memrow = lambda i: (i // nj, 0)
    consts = (wout, nxa, wq, wo, nffn, wr_t, br_col)
    return pl.pallas_call(
        _attn_kernel,
        grid=(t // ts,),
        in_specs=[pl.BlockSpec((ts, d), row)] * 4 + [pl.BlockSpec((mem_len, d), memrow)] * 2
                 + [_spec(c) for c in consts],
        out_specs=[pl.BlockSpec((ts, d), row), pl.BlockSpec((ts, d), row), pl.BlockSpec((LANES, ts), lambda i: (0, i))],
        out_shape=[jax.ShapeDtypeStruct((t, d), F32), jax.ShapeDtypeStruct((t, d), F32),
                   jax.ShapeDtypeStruct((LANES, t), F32)],
        compiler_params=_cparams("arbitrary"),
        name="attn_block",
    )(x2d, ma, mb, mc, k2d, v2d, *consts)


def _first_max(rows):
    best = rows[0]
    for r in rows[1:]:
        best = jnp.maximum(best, r)
    idx = jnp.full(best.shape, len(rows) - 1, jnp.int32)
    for i in range(len(rows) - 2, -1, -1):
        idx = jnp.where(rows[i] >= best, i, idx)
    return best, idx


def _route_kernel(lg_ref, ids_ref, wts_ref, cnt_ref, carry):
    ts = lg_ref.shape[1]
    i = pl.program_id(0)

    @pl.when(i == 0)
    def _():
        carry[...] = jnp.zeros_like(carry)

    lg = lg_ref[...]
    grows = [lg[g:g + 1, :] for g in range(N_EXPERT_GROUPS)]
    gmax, grp = _first_max(grows)
    denom = grows[0] * 0.0
    for r in grows:
        denom = denom + jnp.exp(r - gmax)
    p_grp = 1.0 / denom

    erows = []
    for e in range(EXPERTS_PER_GROUP):
        r_last = N_EXPERT_GROUPS + (N_EXPERT_GROUPS - 1) * EXPERTS_PER_GROUP + e
        sel = lg[r_last:r_last + 1, :]
        for g in range(N_EXPERT_GROUPS - 2, -1, -1):
            r0 = N_EXPERT_GROUPS + g * EXPERTS_PER_GROUP + e
            sel = jnp.where(grp == g, lg[r0:r0 + 1, :], sel)
        erows.append(sel)
    v1, i1 = _first_max(erows)
    rest = [jnp.where(i1 == e, -jnp.inf, erows[e]) for e in range(EXPERTS_PER_GROUP)]
    v2, i2 = _first_max(rest)
    tt = jnp.exp(v2 - v1)
    w1 = p_grp / (1.0 + tt)
    w2 = p_grp * tt / (1.0 + tt)
    e1 = grp * EXPERTS_PER_GROUP + i1
    e2 = grp * EXPERTS_PER_GROUP + i2

    sub = lax.broadcasted_iota(jnp.int32, (N_EXPERTS, ts), 0)
    oh1 = sub == e1
    oh2 = sub == e2
    oh = jnp.where(oh1 | oh2, 1.0, 0.0)
    ti = lax.broadcasted_iota(jnp.int32, (ts, ts), 0)
    tj = lax.broadcasted_iota(jnp.int32, (ts, ts), 1)
    before = (ti < tj).astype(BF16)
    base = carry[...]
    rank = jnp.dot(oh.astype(BF16), before, preferred_element_type=F32) \
        + jnp.concatenate([base] * (ts // LANES), axis=1)
    rank1 = jnp.sum(jnp.where(oh1, rank, 0.0), axis=0, keepdims=True)
    rank2 = jnp.sum(jnp.where(oh2, rank, 0.0), axis=0, keepdims=True)
    carry[...] = base + jnp.sum(oh, axis=1, keepdims=True)

    zi = jnp.zeros((SUBLANES - 4, ts), jnp.int32)
    ids_ref[...] = jnp.concatenate([e1, e2, rank1.astype(jnp.int32), rank2.astype(jnp.int32), zi], axis=0)
    wts_ref[...] = jnp.concatenate([w1, w2, jnp.zeros((SUBLANES - 2, ts), F32)], axis=0)
    cnt_ref[...] = carry[...].astype(jnp.int32)


def _route(logits_t):
    t = logits_t.shape[1]
    ts = min(512, t)
    return pl.pallas_call(
        _route_kernel,
        grid=(t // ts,),
        in_specs=[pl.BlockSpec((LANES, ts), lambda i: (0, i))],
        out_specs=[pl.BlockSpec((SUBLANES, ts), lambda i: (0, i)), pl.BlockSpec((SUBLANES, ts), lambda i: (0, i)),
                   pl.BlockSpec((N_EXPERTS, LANES), lambda i: (0, 0))],
        out_shape=[jax.ShapeDtypeStruct((SUBLANES, t), jnp.int32), jax.ShapeDtypeStruct((SUBLANES, t), F32),
                   jax.ShapeDtypeStruct((N_EXPERTS, LANES), jnp.int32)],
        scratch_shapes=[pltpu.VMEM((N_EXPERTS, LANES), F32)],
        compiler_params=_cparams("arbitrary"),
        name="moe_route",
    )(logits_t)


def _dest_kernel(ids_ref, ps_ref, dst_ref):
    ts = ids_ref.shape[1]
    ids = ids_ref[...]
    sub = lax.broadcasted_iota(jnp.int32, (N_EXPERTS, ts), 0)
    ps = jnp.concatenate([ps_ref[...]] * (ts // LANES), axis=1)
    d1 = jnp.sum(jnp.where(sub == ids[0:1, :], ps, 0), axis=0, keepdims=True) + ids[2:3, :]
    d2 = jnp.sum(jnp.where(sub == ids[1:2, :], ps, 0), axis=0, keepdims=True) + ids[3:4, :]
    dst_ref[...] = jnp.concatenate([d1, d2, jnp.zeros((SUBLANES - 2, ts), jnp.int32)], axis=0)


def _dest(ids, pad_starts_rep):
    t = ids.shape[1]
    ts = min(2048, t)
    return pl.pallas_call(
        _dest_kernel,
        grid=(t // ts,),
        in_specs=[pl.BlockSpec((SUBLANES, ts), lambda i: (0, i)), _full(pad_starts_rep)],
        out_specs=pl.BlockSpec((SUBLANES, ts), lambda i: (0, i)),
        out_shape=jax.ShapeDtypeStruct((SUBLANES, t), jnp.int32),
        compiler_params=_cparams("arbitrary"),
        name="moe_dest",
    )(ids, pad_starts_rep)


def _hbm_row(ref, r):
    return ref.at[pl.ds(r, 1), :]


def _tile_row(ref3, g, u):
    return ref3.at[g, pl.ds(u, 1), :]


def _for_rows(n_rows, fn):
    def trip(g, c):
        for u in range(SUBLANES):
            fn(g, u)
        return c
    lax.fori_loop(0, n_rows // SUBLANES, trip, 0)


def _dispatch_kernel(d1_ref, d2_ref, pe_ref, hf_ref, xs_ref, ring, zbuf, in_sem, out_sem, z_sem, *, n_tiles):
    tg = ring.shape[1]
    ts = tg * SUBLANES
    n_blocks = xs_ref.shape[0] // MOE_BLOCK
    i = pl.program_id(0)
    slot = i % DISPATCH_SLOTS

    def load(tile, s):
        return pltpu.make_async_copy(hf_ref.at[pl.ds(tile * tg, tg)], ring.at[s], in_sem.at[s])

    def drain(s):
        for _ in range(2):
            pltpu.make_async_copy(hf_ref.at[pl.ds(0, tg)], ring.at[s], out_sem.at[s]).wait()

    def zero_block(first_row):
        rows = pl.ds(pl.multiple_of(first_row, MOE_BLOCK), MOE_BLOCK)
        return pltpu.make_async_copy(zbuf, xs_ref.at[rows, :], z_sem)

    @pl.when(i == 0)
    def _():
        load(0, 0).start()
        zbuf[...] = jnp.zeros_like(zbuf)
        for start in (True, False):
            for e in range(N_EXPERTS):
                prev_end = pe_ref[e - 1] if e else 0

                @pl.when(pe_ref[e] > prev_end)
                def _():
                    cp = zero_block(pe_ref[e] - MOE_BLOCK)
                    cp.start() if start else cp.wait()
            for blk in range(n_blocks - N_EXPERTS, n_blocks):
                @pl.when(blk * MOE_BLOCK >= pe_ref[N_EXPERTS - 1])
                def _():
                    cp = zero_block(blk * MOE_BLOCK)
                    cp.start() if start else cp.wait()

    nxt = (i + 1) % DISPATCH_SLOTS

    @pl.when(i + 1 < n_tiles)
    def _():
        @pl.when(i + 1 >= DISPATCH_SLOTS)
        def _():
            drain(nxt)
        load(i + 1, nxt).start()

    load(i, slot).wait()
    base = i * ts
    src = ring.at[slot]
    sem = out_sem.at[slot]

    def scatter(g, u):
        tok = base + g * SUBLANES + u
        pltpu.make_async_copy(_tile_row(src, g, u), _hbm_row(xs_ref, d1_ref[tok]), sem).start(priority=0)
        pltpu.make_async_copy(_tile_row(src, g, u), _hbm_row(xs_ref, d2_ref[tok]), sem).start(priority=1)

    _for_rows(ts, scatter)

    @pl.when(i == n_tiles - 1)
    def _():
        for back in range(min(DISPATCH_SLOTS, n_tiles)):
            drain((n_tiles - 1 - back) % DISPATCH_SLOTS)


def _dispatch(d1, d2, pad_ends, hf, n_slots):
    t, d = hf.shape
    ts = min(256, t)
    n_tiles = t // ts
    hf = hf.reshape(t // SUBLANES, SUBLANES, d)
    grid_spec = pltpu.PrefetchScalarGridSpec(
        num_scalar_prefetch=3,
        grid=(n_tiles,),
        in_specs=[pl.BlockSpec(memory_space=pl.ANY)],
        out_specs=pl.BlockSpec(memory_space=pl.ANY),
        scratch_shapes=[pltpu.VMEM((DISPATCH_SLOTS, ts // SUBLANES, SUBLANES, d), F32), pltpu.VMEM((MOE_BLOCK, d), F32),
                        pltpu.SemaphoreType.DMA((DISPATCH_SLOTS,)), pltpu.SemaphoreType.DMA((DISPATCH_SLOTS,)),
                        pltpu.SemaphoreType.DMA],
    )
    return pl.pallas_call(
        functools.partial(_dispatch_kernel, n_tiles=n_tiles),
        grid_spec=grid_spec,
        out_shape=jax.ShapeDtypeStruct((n_slots, d), F32),
        compiler_params=_cparams("arbitrary"),
        name="moe_dispatch",
    )(d1, d2, pad_ends, hf)


def _expert_kernel(be_ref, nu_ref, xs_ref, wgu_ref, wdn_ref, y_ref, wgu_b, wdn_b):
    b = pl.program_id(0)
    de = wdn_ref.shape[0]

    @pl.when(b < nu_ref[0])
    def _():
        @pl.when((b == 0) | (be_ref[b] != be_ref[jnp.maximum(b - 1, 0)]))
        def _():
            wgu_b[...] = wgu_ref[...].astype(BF16)
            wdn_b[...] = wdn_ref[...].astype(BF16)

        gu = _bdot(xs_ref[...], wgu_b[...])
        act = _silu(gu[:, :de]) * gu[:, de:]
        y_ref[...] = _bdot(act, wdn_b[...])

    @pl.when(b >= nu_ref[0])
    def _():
        y_ref[...] = jnp.zeros_like(y_ref)


def _experts(block_e, n_used, xs, wgu, wdn, layer):
    n_slots, d = xs.shape
    n_blocks = n_slots // MOE_BLOCK

    def blk(b, be, nu):
        return jnp.minimum(b, nu[0] - 1)

    def slab(w):
        return pl.BlockSpec((None, None) + w.shape[2:], lambda b, be, nu: (layer, be[blk(b, be, nu)], 0, 0))

    grid_spec = pltpu.PrefetchScalarGridSpec(
        num_scalar_prefetch=2,
        grid=(n_blocks,),
        in_specs=[pl.BlockSpec((MOE_BLOCK, d), lambda b, be, nu: (blk(b, be, nu), 0)), slab(wgu), slab(wdn)],
        out_specs=pl.BlockSpec((MOE_BLOCK, d), lambda b, be, nu: (b, 0)),
        scratch_shapes=[pltpu.VMEM(wgu.shape[2:], BF16), pltpu.VMEM(wdn.shape[2:], BF16)],
    )
    return pl.pallas_call(
        _expert_kernel,
        grid_spec=grid_spec,
        out_shape=jax.ShapeDtypeStruct((n_slots, d), F32),
        compiler_params=_cparams("arbitrary"),
        name="moe_experts",
    )(block_e, n_used, xs, wgu, wdn)


def _combine_kernel(d1_ref, d2_ref, x_ref, w1_ref, w2_ref, nf_ref, yb_ref, o_ref, buf1, buf2, sem, *, n_tiles):
    ts = x_ref.shape[0]
    i = pl.program_id(0)
    slot = i % 2

    def gather(tile, s):
        base = tile * ts
        b1, b2, sm = buf1.at[s], buf2.at[s], sem.at[s]

        def one(g, u):
            tok = base + g * SUBLANES + u
            pltpu.make_async_copy(_hbm_row(yb_ref, d1_ref[tok]), _tile_row(b1, g, u), sm).start(priority=0)
            pltpu.make_async_copy(_hbm_row(yb_ref, d2_ref[tok]), _tile_row(b2, g, u), sm).start(priority=1)

        _for_rows(ts, one)

    @pl.when(i == 0)
    def _():
        gather(0, 0)

    @pl.when(i + 1 < n_tiles)
    def _():
        gather(i + 1, 1 - slot)

    pltpu.make_async_copy(buf2.at[1 - slot], buf1.at[slot], sem.at[slot]).wait()
    pltpu.make_async_copy(buf1.at[1 - slot], buf2.at[slot], sem.at[slot]).wait()
    d = x_ref.shape[1]
    out = x_ref[...] + w1_ref[...] * buf1[slot].reshape(ts, d) + w2_ref[...] * buf2[slot].reshape(ts, d)
    o_ref[...] = _rms(out, nf_ref[...])


def _combine_final(d1, d2, x2, w1c, w2c, nf, yb):
    t, d = x2.shape
    ts = min(256, t)
    row = lambda i, a, b: (i, 0)
    grid_spec = pltpu.PrefetchScalarGridSpec(
        num_scalar_prefetch=2,
        grid=(t // ts,),
        in_specs=[pl.BlockSpec((ts, d), row), pl.BlockSpec((ts, 1), row), pl.BlockSpec((ts, 1), row),
                  pl.BlockSpec((1, d), lambda i, a, b: (0, 0)), pl.BlockSpec(memory_space=pl.ANY)],
        out_specs=pl.BlockSpec((ts, d), row),
        scratch_shapes=[pltpu.VMEM((2, ts // SUBLANES, SUBLANES, d), F32),
                        pltpu.VMEM((2, ts // SUBLANES, SUBLANES, d), F32), pltpu.SemaphoreType.DMA((2,))],
    )
    return pl.pallas_call(
        functools.partial(_combine_kernel, n_tiles=t // ts),
        grid_spec=grid_spec,
        out_shape=jax.ShapeDtypeStruct((t, d), F32),
        compiler_params=_cparams("arbitrary"),
        name="moe_combine",
    )(d1, d2, x2, w1c, w2c, nf, yb)


def _moe_experts(hf, logits_t, wgu, wdn, layer):
    t, d = hf.shape
    ids, wts, counts = _route(logits_t)
    counts = counts[:, 0]
    padded = (counts + MOE_BLOCK - 1) // MOE_BLOCK * MOE_BLOCK
    pad_ends = jnp.cumsum(padded)
    pad_starts = pad_ends - padded
    n_blocks = (2 * t + MOE_BLOCK - 1) // MOE_BLOCK + N_EXPERTS
    block_start = jnp.arange(n_blocks, dtype=jnp.int32) * MOE_BLOCK
    block_e = jnp.minimum(jnp.sum(pad_ends[None, :] <= block_start[:, None], axis=1), N_EXPERTS - 1).astype(jnp.int32)
    n_used = (pad_ends[-1:] // MOE_BLOCK).astype(jnp.int32)
    dst = _dest(ids, jnp.broadcast_to(pad_starts[:, None].astype(jnp.int32), (N_EXPERTS, LANES)))
    d1, d2 = dst[0], dst[1]
    xs = _dispatch(d1, d2, pad_ends.astype(jnp.int32), hf, n_blocks * MOE_BLOCK)
    yb = _experts(block_e, n_used, xs, wgu, wdn, layer)
    return d1, d2, wts[0].reshape(t, 1), wts[1].reshape(t, 1), yb


def _pair_blockdiag(w):
    z = jnp.zeros_like(w[0::2])
    top = jnp.concatenate([w[0::2], z], axis=2)
    bot = jnp.concatenate([z, w[1::2]], axis=2)
    return jnp.concatenate([top, bot], axis=1)


def kernel(x, mem, norm_mix, w_in, conv_a_w, conv_a_b, lru_gx_w, lru_gx_b, lru_ga_w, lru_ga_b, lru_lambda, sgu_norm, sgu_w, sgu_b, conv_c_w, conv_c_b, ssm_dt_bias, ssm_a_log, ssm_d, ssm_norm, w_a_proj, w_b_proj, w_c_proj, w_out, norm_xa, norm_mem, w_q, w_kv, w_o, norm_ffn, w_router_group, b_router_group, w_router_expert, b_router_expert, w_gate_up, w_down, norm_final):
    bsz, seq, d = x.shape
    mem_len = mem.shape[1]
    depth = w_in.shape[0]
    lru_w = conv_a_w.shape[2]
    sgu_wd = sgu_norm.shape[1]
    ssm_w = ssm_norm.shape[1]
    conv_dim = conv_c_w.shape[2]
    off_ya = lru_w
    off_uv = 2 * lru_w
    off_z = off_uv + 2 * sgu_wd
    off_xbc = off_z + ssm_w
    off_dt = off_xbc + conv_dim
    off_gate = off_dt + SSM_HEADS
    t = bsz * seq
    row = lambda v: v.reshape(1, -1).astype(F32)

    expand = (jnp.arange(LANES)[:, None] == (jnp.arange(ssm_w)[None, :] // SSM_HEAD_DIM)).astype(BF16)

    xcur = x.reshape(t, d)
    mem2d = mem.reshape(bsz * mem_len, d)
    pending = None
    for l in range(depth):
        win = w_in[l].astype(BF16)
        nrm = row(norm_mix[l])
        gates = win[:, off_gate:]
        gw = jnp.concatenate([_pair_blockdiag(lru_gx_w[l]), _pair_blockdiag(lru_ga_w[l])], axis=2).astype(BF16)
        sgu_bias = jnp.repeat(sgu_b[l].T, sgu_wd // SGU_GROUPS, axis=1)
        m_b = _sgu_branch(xcur, seq, nrm, _Cols(win, off_uv, sgu_wd), _Cols(win, off_uv + sgu_wd, sgu_wd),
                          _Cols(gates, d, d), row(sgu_norm[l]), sgu_w[l], sgu_bias, w_b_proj[l].astype(BF16),
                          pending=pending)
        if pending is not None:
            m_b, xcur = m_b
        m_a = _lru_branch(xcur, bsz, seq, nrm, _Cols(win, 0, lru_w), _Cols(win, off_ya, lru_w), _Cols(gates, 0, d),
                          conv_a_w[l], row(conv_a_b[l]), gw, row(lru_gx_b[l]), row(lru_ga_b[l]),
                          row(lru_lambda[l]), w_a_proj[l].astype(BF16))
        pad16 = lambda v: jnp.pad(v.reshape(1, -1), ((0, 0), (0, LANES - SSM_HEADS)))
        m_c = _ssd_branch(xcur, bsz, seq, nrm, _Cols(win, off_z, ssm_w), _Cols(win, off_xbc, ssm_w),
                          _Cols(win, off_xbc + ssm_w, conv_dim - ssm_w), _Cols(win, off_dt, LANES),
                          _Cols(gates, 2 * d, d), conv_c_w[l], row(conv_c_b[l]), pad16(ssm_dt_bias[l]),
                          pad16(ssm_a_log[l]), row(jnp.repeat(ssm_d[l], SSM_HEAD_DIM)), expand,
                          row(ssm_norm[l]), w_c_proj[l].astype(BF16))
        k2d, v2d = _kv_proj(mem2d, row(norm_mem[l]), w_kv[l].astype(BF16))
        n_r = N_EXPERT_GROUPS + N_EXPERTS
        wr_t = jnp.pad(jnp.concatenate([w_router_group[l], w_router_expert[l]], axis=1).T,
                       ((0, LANES - n_r), (0, 0))).astype(BF16)
        br_col = jnp.pad(jnp.concatenate([b_router_group[l], b_router_expert[l]]), (0, LANES - n_r)).reshape(LANES, 1)
        x2, hf, logits_t = _attn_block(xcur, m_a, m_b, m_c, k2d, v2d, bsz, seq, mem_len, w_out[l].astype(BF16),
                                       row(norm_xa[l]), w_q[l].astype(BF16), w_o[l].astype(BF16),
                                       row(norm_ffn[l]), wr_t, br_col)
        xcur = x2
        pending = _moe_experts(hf, logits_t, w_gate_up, w_down, l)
    d1, d2, w1c, w2c, yb = pending
    return _combine_final(d1, d2, xcur, w1c, w2c, row(norm_final), yb).reshape(bsz, seq, d)
```
